```python
import jax, jax.numpy as jnp
from jax import lax
import numpy as np

D_MODEL = 2048
BATCH = 32
SEQ = 256
DEPTH = 4
DEC_BATCH = 2
DEC_SEQ = 1024
PAST_LEN = 512

GRID_W = 64
CHUNK = 128
QBLOCK = 128
A_WIDTH = 1024
A_GROUPS = 8
A_GDIM = A_WIDTH // A_GROUPS
NA_HEADS = 16
NA_HEAD_DIM = 64
NA_WIDTH = NA_HEADS * NA_HEAD_DIM
NA_KH_MAX = 8
NA_KW = 16
D_FF = 4 * D_MODEL
ROPE_THETA = 10000.0
EPS = 1e-6
N_MOD = 6
IN_COLS = 2 * A_WIDTH + 3 * NA_WIDTH + 2 * D_MODEL
SPLITS = [int(s) for s in np.cumsum([A_WIDTH, A_WIDTH, NA_WIDTH, NA_WIDTH, NA_WIDTH, D_MODEL])]

kernel_name = 'hybrid_gmlp_natten_prefix_diffusion_step'


def _rmsnorm(x, g):
    xf = x.astype(jnp.float32)
    y = xf * lax.rsqrt(jnp.mean(xf * xf, axis=-1, keepdims=True) + EPS)
    return (y * g.astype(jnp.float32)).astype(x.dtype)


def _layernorm(x, g, b):
    xf = x.astype(jnp.float32)
    mu = jnp.mean(xf, axis=-1, keepdims=True)
    var = jnp.mean(jnp.square(xf - mu), axis=-1, keepdims=True)
    y = (xf - mu) * lax.rsqrt(var + EPS) * g.astype(jnp.float32) + b.astype(jnp.float32)
    return y.astype(x.dtype)


def _modulation(cvec, w_mod, b_mod):
    m = jax.nn.silu(cvec) @ w_mod + b_mod
    return jnp.split(m[:, None, :], N_MOD, axis=-1)


def _spatial_gating(au, av, ln_g, ln_b, w_s, b_s):
    bsz, seq, _ = au.shape
    u = jax.nn.gelu(au)
    v = _layernorm(jax.nn.gelu(av), ln_g, ln_b)
    v = v.reshape(bsz, seq // CHUNK, CHUNK, A_GROUPS, A_GDIM)
    s = jnp.einsum('gij,bnjgd->bnigd', w_s, v) + b_s.T[None, None, :, :, None]
    return u * s.reshape(bsz, seq, A_WIDTH)


def _heads(x):
    b, l, _ = x.shape
    return x.reshape(b, l, NA_HEADS, NA_HEAD_DIM).transpose(0, 2, 1, 3)


def _merge_heads(x):
    b, h, l, d = x.shape
    return x.transpose(0, 2, 1, 3).reshape(b, l, h * d)


def _rope_axis(x, pos):
    half = x.shape[-1] // 2
    freqs = ROPE_THETA ** (-jnp.arange(half, dtype=jnp.float32) / half)
    ang = pos.astype(jnp.float32)[:, None] * freqs[None, :]
    cos, sin = jnp.cos(ang), jnp.sin(ang)
    xf = x.astype(jnp.float32)
    x1, x2 = xf[..., :half], xf[..., half:]
    return jnp.concatenate([x1 * cos - x2 * sin, x2 * cos + x1 * sin], axis=-1).astype(x.dtype)


def _rope_2d(x):
    t = jnp.arange(x.shape[-2])
    hd = NA_HEAD_DIM // 2
    return jnp.concatenate([_rope_axis(x[..., :hd], t // GRID_W), _rope_axis(x[..., hd:], t % GRID_W)], axis=-1)


def _context_attention(q, k, v):
    b, h, l, d = q.shape
    nb = l // QBLOCK
    scale = d ** -0.5
    qb = q.reshape(b, h, nb, QBLOCK, d).transpose(2, 0, 1, 3, 4)

    def block(qi):
        s = jnp.einsum('bhqd,bhkd->bhqk', qi, k).astype(jnp.float32) * scale
        p = jax.nn.softmax(s, axis=-1).astype(v.dtype)
        return jnp.einsum('bhqk,bhkd->bhqd', p, v)

    o = lax.map(block, qb)
    return o.transpose(1, 2, 0, 3, 4).reshape(b, h, l, d)


def _neighbourhood_attention(q, k, v, k_ctx, v_ctx, rpb):
    b, h, t, d = q.shape
    rows = t // GRID_W
    kh = min(NA_KH_MAX, rows)
    scale = d ** -0.5
    qg = q.reshape(b, h, rows, GRID_W, d)
    kg = k.reshape(b, h, rows, GRID_W, d)
    vg = v.reshape(b, h, rows, GRID_W, d)
    cols = np.arange(GRID_W)
    cstart = np.clip(cols - NA_KW // 2, 0, GRID_W - NA_KW)
    col_mask = jnp.asarray((cols[None, :] >= cstart[:, None]) & (cols[None, :] < cstart[:, None] + NA_KW))
    col_idx = jnp.asarray(np.clip(cols[None, :] - cols[:, None] + NA_KW - 1, 0, 2 * NA_KW - 2))

    def row_block(r):
        rs = jnp.clip(r - kh // 2, 0, rows - kh)
        k_rows = lax.dynamic_slice_in_dim(kg, rs, kh, axis=2)
        v_rows = lax.dynamic_slice_in_dim(vg, rs, kh, axis=2)
        q_row = lax.dynamic_index_in_dim(qg, r, axis=2, keepdims=False)
        row_idx = rs + jnp.arange(kh) - r + NA_KH_MAX - 1
        bias = rpb[:, row_idx][:, :, col_idx].transpose(0, 2, 1, 3)
        s_win = jnp.einsum('bhqd,bhiwd->bhqiw', q_row, k_rows).astype(jnp.float32) * scale + bias.astype(jnp.float32)
        s_win = jnp.where(col_mask[:, None, :], s_win, -jnp.inf)
        s_ctx = jnp.einsum('bhqd,bhkd->bhqk', q_row, k_ctx).astype(jnp.float32) * scale
        s = jnp.concatenate([s_win.reshape(b, h, GRID_W, kh * GRID_W), s_ctx], axis=-1)
        p = jax.nn.softmax(s, axis=-1).astype(v.dtype)
        p_win = p[..., :kh * GRID_W].reshape(b, h, GRID_W, kh, GRID_W)
        p_ctx = p[..., kh * GRID_W:]
        return jnp.einsum('bhqiw,bhiwd->bhqd', p_win, v_rows) + jnp.einsum('bhqk,bhkd->bhqd', p_ctx, v_ctx)

    o = lax.map(row_block, jnp.arange(rows))
    return o.transpose(1, 2, 0, 3, 4).reshape(b, h, t, d)


def _layer(x, cvec, p, attend):
    shift1, scale1, gate1, shift2, scale2, gate2 = _modulation(cvec, p['w_mod'], p['b_mod'])
    h = _rmsnorm(x, p['g_pre_mix']) * (1 + scale1) + shift1
    au, av, q, k, v, ga, gb = jnp.split(h @ p['w_in'], SPLITS, axis=-1)
    ya = _spatial_gating(au, av, p['sgu_ln_g'], p['sgu_ln_b'], p['sgu_w'], p['sgu_b'])
    yb, extra = attend(_heads(q), _heads(k), _heads(v))
    merged = jax.nn.sigmoid(ga) * (ya @ p['w_pa']) + jax.nn.sigmoid(gb) * (_merge_heads(yb) @ p['w_pb'])
    x = x + gate1 * _rmsnorm(merged @ p['w_out'], p['g_post_mix'])
    h = _rmsnorm(x, p['g_pre_ffn']) * (1 + scale2) + shift2
    f = jnp.square(jax.nn.relu(h @ p['w_ff1'])) @ p['w_ff2']
    x = x + gate2 * _rmsnorm(f, p['g_post_ffn'])
    return x, extra


def setup_inputs(seed: int = 0) -> dict:
    key = jax.random.key(seed)
    ks = jax.random.split(key, 24)
    f32 = jnp.float32
    nrm = lambda k, shape, s: jax.random.normal(k, shape, f32) * s
    return {
        'x_prompt': nrm(ks[0], (BATCH, SEQ, D_MODEL), 1.0),
        'x_sample': nrm(ks[1], (DEC_BATCH, DEC_SEQ, D_MODEL), 1.0),
        'cache_ctx_k': nrm(ks[2], (DEC_BATCH, DEPTH, NA_HEADS, PAST_LEN, NA_HEAD_DIM), 1.0),
        'cache_ctx_v': nrm(ks[3], (DEC_BATCH, DEPTH, NA_HEADS, PAST_LEN, NA_HEAD_DIM), 1.0),
        'c': nrm(ks[4], (DEC_BATCH, D_MODEL), 1.0),
        'c_ctx': nrm(ks[5], (D_MODEL,), 1.0),
        'w_mod': nrm(ks[6], (DEPTH, D_MODEL, N_MOD * D_MODEL), 0.5 * D_MODEL ** -0.5),
        'b_mod': nrm(ks[7], (DEPTH, N_MOD * D_MODEL), 0.01),
        'g_pre_mix': 1.0 + nrm(ks[8], (DEPTH, D_MODEL), 0.01),
        'g_post_mix': 1.0 + nrm(ks[9], (DEPTH, D_MODEL), 0.01),
        'g_pre_ffn': 1.0 + nrm(ks[10], (DEPTH, D_MODEL), 0.01),
        'g_post_ffn': 1.0 + nrm(ks[11], (DEPTH, D_MODEL), 0.01),
        'w_in': nrm(ks[12], (DEPTH, D_MODEL, IN_COLS), D_MODEL ** -0.5),
        'sgu_ln_g': 1.0 + nrm(ks[13], (DEPTH, A_WIDTH), 0.01),
        'sgu_ln_b': nrm(ks[14], (DEPTH, A_WIDTH), 0.01),
        'sgu_w': nrm(ks[15], (DEPTH, A_GROUPS, CHUNK, CHUNK), CHUNK ** -0.5),
        'sgu_b': 1.0 + nrm(ks[16], (DEPTH, A_GROUPS, CHUNK), 0.01),
        'na_rpb': nrm(ks[17], (DEPTH, NA_HEADS, 2 * NA_KH_MAX - 1, 2 * NA_KW - 1), 0.1),
        'w_pa': nrm(ks[18], (DEPTH, A_WIDTH, D_MODEL), A_WIDTH ** -0.5),
        'w_pb': nrm(ks[19], (DEPTH, NA_WIDTH, D_MODEL), NA_WIDTH ** -0.5),
        'w_out': nrm(ks[20], (DEPTH, D_MODEL, D_MODEL), D_MODEL ** -0.5),
        'w_ff1': nrm(ks[21], (DEPTH, D_MODEL, D_FF), D_MODEL ** -0.5),
        'w_ff2': nrm(ks[22], (DEPTH, D_FF, D_MODEL), D_FF ** -0.5),
    }


def reference(x_prompt, x_sample, cache_ctx_k, cache_ctx_v, c, c_ctx, w_mod, b_mod, g_pre_mix, g_post_mix,
              g_pre_ffn, g_post_ffn, w_in, sgu_ln_g, sgu_ln_b, sgu_w, sgu_b, na_rpb, w_pa, w_pb, w_out,
              w_ff1, w_ff2):
    xc = x_prompt
    xs = x_sample
    ks_out, vs_out = [], []
    for l in range(DEPTH):
        p = {'w_mod': w_mod[l], 'b_mod': b_mod[l], 'g_pre_mix': g_pre_mix[l], 'g_post_mix': g_post_mix[l],
             'g_pre_ffn': g_pre_ffn[l], 'g_post_ffn': g_post_ffn[l], 'w_in': w_in[l], 'sgu_ln_g': sgu_ln_g[l],
             'sgu_ln_b': sgu_ln_b[l], 'sgu_w': sgu_w[l], 'sgu_b': sgu_b[l], 'w_pa': w_pa[l], 'w_pb': w_pb[l],
             'w_out': w_out[l], 'w_ff1': w_ff1[l], 'w_ff2': w_ff2[l]}
        rpb = na_rpb[l]
        xc, (k_l, v_l) = _layer(xc, c_ctx[None, :], p, lambda q, k, v: (_context_attention(q, k, v), (k, v)))
        ks_out.append(k_l)
        vs_out.append(v_l)
        ck = cache_ctx_k[:, l]
        cv = cache_ctx_v[:, l]
        xs, _ = _layer(xs, c, p, lambda q, k, v, ck=ck, cv=cv, rpb=rpb: (
            _neighbourhood_attention(_rope_2d(q), _rope_2d(k), v, ck, cv, rpb), None))
    state_ctx_k = jnp.stack(ks_out, axis=1)
    state_ctx_v = jnp.stack(vs_out, axis=1)
    return (xc, xs, state_ctx_k, state_ctx_v)
```

```python
import functools

import numpy as np
import jax
import jax.numpy as jnp
from jax import lax
from jax.experimental import pallas as pl
from jax.experimental.pallas import tpu as pltpu

D_MODEL = 2048
BATCH = 32
SEQ = 256
DEPTH = 4
DEC_BATCH = 2
DEC_SEQ = 1024
PAST_LEN = 512
GRID_W = 64
CHUNK = 128
A_WIDTH = 1024
A_GROUPS = 8
NA_HEADS = 16
NA_HEAD_DIM = 64
NA_WIDTH = NA_HEADS * NA_HEAD_DIM
NA_KH_MAX = 8
NA_KW = 16
D_FF = 4 * D_MODEL
ROPE_THETA = 10000.0
EPS = 1e-6
N_MOD = 6
IN_COLS = 2 * A_WIDTH + 3 * NA_WIDTH + 2 * D_MODEL

N_CTX_TOK = BATCH * SEQ
N_TOK = N_CTX_TOK + DEC_BATCH * DEC_SEQ
ROWS = DEC_SEQ // GRID_W
NA_KH = min(NA_KH_MAX, ROWS)
MOD_ROWS = 8
LANES = 128
HEADS_PER_BLOCK = LANES // NA_HEAD_DIM
N_HEAD_BLOCKS = NA_HEADS // HEADS_PER_BLOCK

NA_QROWS = 2
NA_QBLK = NA_QROWS * GRID_W
NA_WROWS = 10
NA_WBLK = NA_WROWS * GRID_W
NA_NPAIR = ROWS // NA_QROWS

VMEM_LIMIT = 56 * 1024 * 1024

_f32 = jnp.float32
_bf16 = jnp.bfloat16
_NT = (((1,), (1,)), ((), ()))


def _na_window_classes():
    starts, cls_of_pair, sigs = [], [], []
    for p in range(NA_NPAIR):
        start = int(np.clip(NA_QROWS * p - NA_KH // 2, 0, ROWS - NA_WROWS))
        start -= start % 2
        sig = []
        for rr in range(NA_QROWS):
            r = NA_QROWS * p + rr
            rs = int(np.clip(r - NA_KH // 2, 0, ROWS - NA_KH))
            assert start <= rs and rs + NA_KH <= start + NA_WROWS
            sig.append((r - start, rs - start))
        sig = tuple(sig)
        if sig not in sigs:
            sigs.append(sig)
        starts.append(start)
        cls_of_pair.append(sigs.index(sig))
    return starts, cls_of_pair, sigs


NA_STARTS, NA_CLS, NA_SIGS = _na_window_classes()
NA_NCLS = len(NA_SIGS)


def _mod_row(tile, tm):
    return jnp.maximum((tile * tm - N_CTX_TOK) // DEC_SEQ + 1, 0)


def _sigmoid(x):
    return 1.0 / (1.0 + jnp.exp(-x))


def _gelu(x):
    return 0.5 * x * (1.0 + jnp.tanh(np.sqrt(2.0 / np.pi).astype(np.float32) * (x + 0.044715 * (x * x * x))))


def _modnorm(x, g, scale, shift):
    y = x * lax.rsqrt(jnp.mean(x * x, axis=-1, keepdims=True) + EPS)
    return (y * g) * (1.0 + scale) + shift


def _rmsnorm(x, g):
    return (x * lax.rsqrt(jnp.mean(x * x, axis=-1, keepdims=True) + EPS)) * g


def _mod_kernel(c_ref, w_ref, b_ref, o_ref):
    c = c_ref[...]
    s = (c * _sigmoid(c)).astype(_bf16)
    o_ref[0] = jnp.dot(s, w_ref[0].astype(_bf16), preferred_element_type=_f32) + b_ref[0]


def _modulation(cvec, w_mod, b_mod):
    tn = 1024
    n_cols = N_MOD * D_MODEL
    return pl.pallas_call(
        _mod_kernel,
        out_shape=jax.ShapeDtypeStruct((DEPTH, MOD_ROWS, n_cols), _f32),
        grid=(DEPTH, n_cols // tn),
        in_specs=[
            pl.BlockSpec((MOD_ROWS, D_MODEL), lambda l, j: (0, 0)),
            pl.BlockSpec((1, D_MODEL, tn), lambda l, j: (l, 0, j)),
            pl.BlockSpec((1, 1, tn), lambda l, j: (l, 0, j)),
        ],
        out_specs=pl.BlockSpec((1, MOD_ROWS, tn), lambda l, j: (l, 0, j)),
        compiler_params=pltpu.CompilerParams(
            dimension_semantics=("arbitrary", "arbitrary"), vmem_limit_bytes=VMEM_LIMIT),
        name="modulation",
    )(cvec, w_mod, b_mod.reshape(DEPTH, 1, n_cols))


def _toeplitz_kernel(rpb_ref, onehot_ref, mask_ref, o_ref):
    o_ref[...] = jnp.dot(rpb_ref[...], onehot_ref[...], preferred_element_type=_f32,
                         precision=lax.Precision.HIGHEST) + mask_ref[...]


def _na_bias_tables(na_rpb):
    n_row = 2 * NA_KH_MAX - 1
    n_col = 2 * NA_KW - 1
    n_col_pad = 32
    cols = np.arange(GRID_W)
    cstart = np.clip(cols - NA_KW // 2, 0, GRID_W - NA_KW)
    in_win = (cols[None, :] >= cstart[:, None]) & (cols[None, :] < cstart[:, None] + NA_KW)
    col_idx = np.clip(cols[None, :] - cols[:, None] + NA_KW - 1, 0, n_col - 1)
    onehot = (col_idx.reshape(1, -1) == np.arange(n_col_pad)[:, None]) & in_win.reshape(1, -1)
    mask = np.where(in_win.reshape(1, -1), 0.0, -np.inf).astype(np.float32)
    rpb2 = jnp.pad(na_rpb.reshape(DEPTH * NA_HEADS * n_row, n_col), ((0, 0), (0, n_col_pad - n_col)))
    toep = pl.pallas_call(
        _toeplitz_kernel,
        out_shape=jax.ShapeDtypeStruct((DEPTH * NA_HEADS * n_row, GRID_W * GRID_W), _f32),
        compiler_params=pltpu.CompilerParams(vmem_limit_bytes=VMEM_LIMIT),
        name="rpb_toeplitz",
    )(rpb2, jnp.asarray(onehot.astype(np.float32)), jnp.asarray(mask))
    toep = toep.reshape(DEPTH, NA_HEADS, n_row, GRID_W, GRID_W)
    a_idx = np.zeros((NA_NCLS, NA_QROWS, NA_WROWS), np.int32)
    valid = np.zeros((NA_NCLS, NA_QROWS, NA_WROWS), bool)
    for c, sig in enumerate(NA_SIGS):
        for rr, (r_rel, rs_rel) in enumerate(sig):
            for i in range(NA_WROWS):
                valid[c, rr, i] = rs_rel <= i < rs_rel + NA_KH
                a_idx[c, rr, i] = np.clip(i - r_rel + NA_KH_MAX - 1, 0, n_row - 1)
    blocks = toep[:, :, a_idx.reshape(-1)].reshape(DEPTH, NA_HEADS, NA_NCLS, NA_QROWS, NA_WROWS, GRID_W, GRID_W)
    blocks = jnp.where(jnp.asarray(valid)[None, None, :, :, :, None, None], blocks, -jnp.inf)
    blocks = blocks.transpose(0, 1, 2, 3, 5, 4, 6)
    return blocks.reshape(DEPTH, N_HEAD_BLOCKS, HEADS_PER_BLOCK, NA_NCLS, NA_QBLK, NA_WBLK)


def _rope_tables():
    t = jnp.arange(DEC_SEQ)
    half = NA_HEAD_DIM // 4
    freqs = ROPE_THETA ** (-jnp.arange(half, dtype=_f32) / half)
    ang_r = (t // GRID_W).astype(_f32)[:, None] * freqs[None, :]
    ang_c = (t % GRID_W).astype(_f32)[:, None] * freqs[None, :]
    cos_h = jnp.concatenate([jnp.cos(ang_r)] * 2 + [jnp.cos(ang_c)] * 2, axis=-1)
    sin_h = jnp.concatenate([-jnp.sin(ang_r), jnp.sin(ang_r), -jnp.sin(ang_c), jnp.sin(ang_c)], axis=-1)
    return jnp.tile(cos_h, (1, HEADS_PER_BLOCK)), jnp.tile(sin_h, (1, HEADS_PER_BLOCK))


IN_TM = 1024
IN_TN = 1024
IN_PROJ_COLS = 2 * A_WIDTH + 3 * NA_WIDTH + 2 * D_MODEL
Q_COL_TILE = (2 * A_WIDTH) // IN_TN
K_COL_TILE = (2 * A_WIDTH + NA_WIDTH) // IN_TN
ROW_CHUNK = 256


def _in_kernel(x_ref, shift_ref, scale_ref, g_ref, w_ref, cos_ref, sin_ref, o_ref, h_ref):
    i = pl.program_id(0)
    j = pl.program_id(1)

    @pl.when(j == 0)
    def _():
        def body(c, carry):
            rows = pl.ds(pl.multiple_of(c * ROW_CHUNK, ROW_CHUNK), ROW_CHUNK)
            h = _modnorm(x_ref[rows, :], g_ref[...], scale_ref[0], shift_ref[0])
            h_ref[rows, :] = h.astype(_bf16)
            return carry
        lax.fori_loop(0, IN_TM // ROW_CHUNK, body, 0)

    acc = jnp.dot(h_ref[...], w_ref[...], preferred_element_type=_f32)
    is_rope = jnp.logical_and(i >= N_CTX_TOK // IN_TM, jnp.logical_or(j == Q_COL_TILE, j == K_COL_TILE))

    @pl.when(is_rope)
    def _():
        lane = lax.broadcasted_iota(jnp.int32, (IN_TM, LANES), 1)
        first_half = (lane % (NA_HEAD_DIM // 2)) < (NA_HEAD_DIM // 4)
        cos = cos_ref[...]
        sin = sin_ref[...]
        for cb in range(IN_TN // LANES):
            xb = acc[:, cb * LANES:(cb + 1) * LANES]
            partner = jnp.where(first_half, pltpu.roll(xb, LANES - NA_HEAD_DIM // 4, axis=1),
                                pltpu.roll(xb, NA_HEAD_DIM // 4, axis=1))
            o_ref[:, cb * LANES:(cb + 1) * LANES] = xb * cos + partner * sin

    @pl.when(jnp.logical_not(is_rope))
    def _():
        o_ref[...] = acc


def _in_proj(x, mod_l, g, w_in, cos_t, sin_t):
    n_m = N_TOK // IN_TM
    mod_spec = lambda chunk: pl.BlockSpec((1, 1, D_MODEL), lambda i, j: (_mod_row(i, IN_TM), 0, chunk))
    return pl.pallas_call(
        _in_kernel,
        out_shape=jax.ShapeDtypeStruct((N_TOK, IN_PROJ_COLS), _f32),
        grid=(n_m, IN_PROJ_COLS // IN_TN),
        in_specs=[
            pl.BlockSpec((IN_TM, D_MODEL), lambda i, j: (i, 0)),
            mod_spec(0), mod_spec(1),
            pl.BlockSpec((1, D_MODEL), lambda i, j: (0, 0)),
            pl.BlockSpec((D_MODEL, IN_TN), lambda i, j: (0, j)),
            pl.BlockSpec((DEC_SEQ, LANES), lambda i, j: (0, 0)),
            pl.BlockSpec((DEC_SEQ, LANES), lambda i, j: (0, 0)),
        ],
        out_specs=pl.BlockSpec((IN_TM, IN_TN), lambda i, j: (i, j)),
        scratch_shapes=[pltpu.VMEM((IN_TM, D_MODEL), _bf16)],
        compiler_params=pltpu.CompilerParams(
            dimension_semantics=("arbitrary", "arbitrary"), vmem_limit_bytes=VMEM_LIMIT),
        name="in_proj",
    )(x, mod_l, mod_l, g, w_in, cos_t, sin_t)


SGU_TM = 1024


def _sgu_kernel(au_ref, av_ref, lng_ref, lnb_ref, ws_ref, bs_ref, o_ref):
    gdim = A_WIDTH // A_GROUPS

    def body(c, carry):
        rows = pl.ds(pl.multiple_of(c * CHUNK, CHUNK), CHUNK)
        u = _gelu(au_ref[rows, :])
        gv = _gelu(av_ref[rows, :])
        mu = jnp.mean(gv, axis=-1, keepdims=True)
        var = jnp.mean(jnp.square(gv - mu), axis=-1, keepdims=True)
        vn = ((gv - mu) * lax.rsqrt(var + EPS) * lng_ref[...] + lnb_ref[...]).astype(_bf16)
        for g in range(A_GROUPS):
            cols = slice(g * gdim, (g + 1) * gdim)
            s = jnp.dot(ws_ref[g], vn[:, cols], preferred_element_type=_f32) + bs_ref[g]
            o_ref[rows, cols] = (u[:, cols] * s).astype(_bf16)
        return carry

    lax.fori_loop(0, SGU_TM // CHUNK, body, 0)


def _sgu(proj, ln_g, ln_b, w_s, b_s):
    gdim = A_WIDTH // A_GROUPS
    return pl.pallas_call(
        _sgu_kernel,
        out_shape=jax.ShapeDtypeStruct((N_TOK, A_WIDTH), _bf16),
        grid=(N_TOK // SGU_TM,),
        in_specs=[
            pl.BlockSpec((SGU_TM, A_WIDTH), lambda i: (i, 0)),
            pl.BlockSpec((SGU_TM, A_WIDTH), lambda i: (i, 1)),
            pl.BlockSpec((1, A_WIDTH), lambda i: (0, 0)),
            pl.BlockSpec((1, A_WIDTH), lambda i: (0, 0)),
            pl.BlockSpec((A_GROUPS, CHUNK, CHUNK), lambda i: (0, 0, 0)),
            pl.BlockSpec((A_GROUPS, CHUNK, gdim), lambda i: (0, 0, 0)),
        ],
        out_specs=pl.BlockSpec((SGU_TM, A_WIDTH), lambda i: (i, 0)),
        compiler_params=pltpu.CompilerParams(dimension_semantics=("arbitrary",), vmem_limit_bytes=VMEM_LIMIT),
        name="sgu",
    )(proj, proj, ln_g, ln_b, w_s, b_s)


ATT_SCALE = NA_HEAD_DIM ** -0.5
Q_LANE_BLOCK = (2 * A_WIDTH) // LANES
K_LANE_BLOCK = (2 * A_WIDTH + NA_WIDTH) // LANES
V_LANE_BLOCK = (2 * A_WIDTH + 2 * NA_WIDTH) // LANES


def _softmax_pv(parts):
    m = functools.reduce(jnp.maximum, [jnp.max(s, axis=-1, keepdims=True) for s, _ in parts])
    es = [jnp.exp(s - m) for s, _ in parts]
    l = functools.reduce(jnp.add, [jnp.sum(e, axis=-1, keepdims=True) for e in es])
    o = functools.reduce(jnp.add, [jnp.dot(e.astype(_bf16), v, preferred_element_type=_f32)
                                   for e, (_, v) in zip(es, parts)])
    return o / l


def _ctx_attn_kernel(q_ref, k_ref, v_ref, o_ref):
    outs = []
    for hh in range(HEADS_PER_BLOCK):
        cols = slice(hh * NA_HEAD_DIM, (hh + 1) * NA_HEAD_DIM)
        q = (q_ref[:, cols] * ATT_SCALE).astype(_bf16)
        k = k_ref[:, cols].astype(_bf16)
        v = v_ref[:, cols].astype(_bf16)
        s = lax.dot_general(q, k, _NT, preferred_element_type=_f32)
        outs.append(_softmax_pv([(s, v)]))
    o_ref[...] = jnp.concatenate(outs, axis=-1).astype(_bf16)


def _ctx_attention(proj):
    return pl.pallas_call(
        _ctx_attn_kernel,
        out_shape=jax.ShapeDtypeStruct((N_CTX_TOK, NA_WIDTH), _bf16),
        grid=(BATCH, N_HEAD_BLOCKS),
        in_specs=[
            pl.BlockSpec((SEQ, LANES), lambda b, h: (b, Q_LANE_BLOCK + h)),
            pl.BlockSpec((SEQ, LANES), lambda b, h: (b, K_LANE_BLOCK + h)),
            pl.BlockSpec((SEQ, LANES), lambda b, h: (b, V_LANE_BLOCK + h)),
        ],
        out_specs=pl.BlockSpec((SEQ, LANES), lambda b, h: (b, h)),
        compiler_params=pltpu.CompilerParams(
            dimension_semantics=("arbitrary", "arbitrary"), vmem_limit_bytes=VMEM_LIMIT),
        name="ctx_attention",
    )(proj, proj, proj)


def _na_attn_kernel(q_ref, k_ref, v_ref, ck_ref, cv_ref, bias_ref, o_ref):
    head_outs = []
    for hh in range(HEADS_PER_BLOCK):
        pair_outs = []
        cols = slice(hh * NA_HEAD_DIM, (hh + 1) * NA_HEAD_DIM)
        q = (q_ref[:, cols] * ATT_SCALE).astype(_bf16)
        k = k_ref[:, cols].astype(_bf16)
        v = v_ref[:, cols].astype(_bf16)
        ck = ck_ref[0, 0, hh].astype(_bf16)
        cv = cv_ref[0, 0, hh].astype(_bf16)
        s_ctx = lax.dot_general(q, ck, _NT, preferred_element_type=_f32)
        for p in range(NA_NPAIR):
            qrows = slice(p * NA_QBLK, (p + 1) * NA_QBLK)
            krows = slice(NA_STARTS[p] * GRID_W, NA_STARTS[p] * GRID_W + NA_WBLK)
            s_win = lax.dot_general(q[qrows], k[krows], _NT, preferred_element_type=_f32)
            s_win = s_win + bias_ref[0, 0, hh, NA_CLS[p]]
            pair_outs.append(_softmax_pv([(s_win, v[krows]), (s_ctx[qrows], cv)]))
        head_outs.append(jnp.concatenate(pair_outs, axis=0))
    o_ref[...] = jnp.concatenate(head_outs, axis=-1).astype(_bf16)


def _na_attention(proj, cache_k, cache_v, bias_tab, layer):
    tile0 = N_CTX_TOK // DEC_SEQ
    qkv_spec = lambda lane_block: pl.BlockSpec((DEC_SEQ, LANES), lambda h, b: (tile0 + b, lane_block + h))
    cache_spec = pl.BlockSpec((1, 1, HEADS_PER_BLOCK, PAST_LEN, NA_HEAD_DIM), lambda h, b: (b, layer, h, 0, 0))
    return pl.pallas_call(
        _na_attn_kernel,
        out_shape=jax.ShapeDtypeStruct((DEC_BATCH * DEC_SEQ, NA_WIDTH), _bf16),
        grid=(N_HEAD_BLOCKS, DEC_BATCH),
        in_specs=[
            qkv_spec(Q_LANE_BLOCK), qkv_spec(K_LANE_BLOCK), qkv_spec(V_LANE_BLOCK),
            cache_spec, cache_spec,
            pl.BlockSpec((1, 1, HEADS_PER_BLOCK, NA_NCLS, NA_QBLK, NA_WBLK), lambda h, b: (layer, h, 0, 0, 0, 0)),
        ],
        out_specs=pl.BlockSpec((DEC_SEQ, LANES), lambda h, b: (b, h)),
        compiler_params=pltpu.CompilerParams(
            dimension_semantics=("arbitrary", "arbitrary"), vmem_limit_bytes=VMEM_LIMIT),
        name="na_attention",
    )(proj, proj, proj, cache_k, cache_v, bias_tab)


MIX_TM = 512
MIX_TC = 512
GA_COL_TILE = (2 * A_WIDTH + 3 * NA_WIDTH) // MIX_TC
GB_COL_TILE = GA_COL_TILE + D_MODEL // MIX_TC


def _mix_kernel(ga_ref, gb_ref, ya_ref, yb_ref, wpa_ref, wpb_ref, wout_ref, x_ref, gate_ref, g_ref, o_ref):
    j = pl.program_id(1)
    pa = jnp.dot(ya_ref[...], wpa_ref[...], preferred_element_type=_f32)
    pb = jnp.dot(yb_ref[...], wpb_ref[...], preferred_element_type=_f32)
    merged = (_sigmoid(ga_ref[...]) * pa + _sigmoid(gb_ref[...]) * pb).astype(_bf16)
    part = jnp.dot(merged, wout_ref[...], preferred_element_type=_f32)

    @pl.when(j == 0)
    def _():
        o_ref[...] = part

    @pl.when(j > 0)
    def _():
        o_ref[...] += part

    @pl.when(j == pl.num_programs(1) - 1)
    def _():
        o_ref[...] = x_ref[...] + gate_ref[0] * _rmsnorm(o_ref[...], g_ref[...])


def _mix(proj, ya, yb, w_pa, w_pb, w_out, x, mod_l, g_post):
    return pl.pallas_call(
        _mix_kernel,
        out_shape=jax.ShapeDtypeStruct((N_TOK, D_MODEL), _f32),
        grid=(N_TOK // MIX_TM, D_MODEL // MIX_TC),
        in_specs=[
            pl.BlockSpec((MIX_TM, MIX_TC), lambda i, j: (i, GA_COL_TILE + j)),
            pl.BlockSpec((MIX_TM, MIX_TC), lambda i, j: (i, GB_COL_TILE + j)),
            pl.BlockSpec((MIX_TM, A_WIDTH), lambda i, j: (i, 0)),
            pl.BlockSpec((MIX_TM, NA_WIDTH), lambda i, j: (i, 0)),
            pl.BlockSpec((A_WIDTH, MIX_TC), lambda i, j: (0, j)),
            pl.BlockSpec((NA_WIDTH, MIX_TC), lambda i, j: (0, j)),
            pl.BlockSpec((MIX_TC, D_MODEL), lambda i, j: (j, 0)),
            pl.BlockSpec((MIX_TM, D_MODEL), lambda i, j: (i, 0)),
            pl.BlockSpec((1, 1, D_MODEL), lambda i, j: (_mod_row(i, MIX_TM), 0, 2)),
            pl.BlockSpec((1, D_MODEL), lambda i, j: (0, 0)),
        ],
        out_specs=pl.BlockSpec((MIX_TM, D_MODEL), lambda i, j: (i, 0)),
        compiler_params=pltpu.CompilerParams(
            dimension_semantics=("arbitrary", "arbitrary"), vmem_limit_bytes=VMEM_LIMIT),
        name="mix_out",
    )(proj, proj, ya, yb, w_pa, w_pb, w_out, x, mod_l, g_post)


FFN_TM = 1024
FFN_TF = 512


FFN_OUT_SPLIT = 2


def _ffn_kernel(x_ref, shift_ref, scale_ref, gate_ref, gpre_ref, gpost_ref, w1_ref, w2_ref, o_ref, h_ref):
    k = pl.program_id(1)

    @pl.when(k == 0)
    def _():
        def body(c, carry):
            rows = pl.ds(pl.multiple_of(c * ROW_CHUNK, ROW_CHUNK), ROW_CHUNK)
            h = _modnorm(x_ref[rows, :], gpre_ref[...], scale_ref[0], shift_ref[0])
            h_ref[rows, :] = h.astype(_bf16)
            return carry
        lax.fori_loop(0, FFN_TM // ROW_CHUNK, body, 0)

    a = jnp.dot(h_ref[...], w1_ref[...], preferred_element_type=_f32)
    a = jnp.square(jnp.maximum(a, 0.0)).astype(_bf16)
    width = D_MODEL // FFN_OUT_SPLIT
    for s in range(FFN_OUT_SPLIT):
        cols = slice(s * width, (s + 1) * width)
        part = jnp.dot(a, w2_ref[:, cols], preferred_element_type=_f32)

        @pl.when(k == 0)
        def _():
            o_ref[:, cols] = part

        @pl.when(k > 0)
        def _():
            o_ref[:, cols] += part

    @pl.when(k == pl.num_programs(1) - 1)
    def _():
        def body(c, carry):
            rows = pl.ds(pl.multiple_of(c * ROW_CHUNK, ROW_CHUNK), ROW_CHUNK)
            o_ref[rows, :] = x_ref[rows, :] + gate_ref[0] * _rmsnorm(o_ref[rows, :], gpost_ref[...])
            return carry
        lax.fori_loop(0, FFN_TM // ROW_CHUNK, body, 0)


def _ffn(x, mod_l, g_pre, g_post, w1, w2):
    mod_spec = lambda chunk: pl.BlockSpec((1, 1, D_MODEL), lambda i, k: (_mod_row(i, FFN_TM), 0, chunk))
    return pl.pallas_call(
        _ffn_kernel,
        out_shape=jax.ShapeDtypeStruct((N_TOK, D_MODEL), _f32),
        grid=(N_TOK // FFN_TM, D_FF // FFN_TF),
        in_specs=[
            pl.BlockSpec((FFN_TM, D_MODEL), lambda i, k: (i, 0)),
            mod_spec(3), mod_spec(4), mod_spec(5),
            pl.BlockSpec((1, D_MODEL), lambda i, k: (0, 0)),
            pl.BlockSpec((1, D_MODEL), lambda i, k: (0, 0)),
            pl.BlockSpec((D_MODEL, FFN_TF), lambda i, k: (0, k)),
            pl.BlockSpec((FFN_TF, D_MODEL), lambda i, k: (k, 0)),
        ],
        out_specs=pl.BlockSpec((FFN_TM, D_MODEL), lambda i, k: (i, 0)),
        scratch_shapes=[pltpu.VMEM((FFN_TM, D_MODEL), _bf16)],
        compiler_params=pltpu.CompilerParams(
            dimension_semantics=("arbitrary", "arbitrary"), vmem_limit_bytes=VMEM_LIMIT),
        name="ffn",
    )(x, mod_l, mod_l, mod_l, g_pre, g_post, w1, w2)


def kernel(x_prompt, x_sample, cache_ctx_k, cache_ctx_v, c, c_ctx, w_mod, b_mod, g_pre_mix, g_post_mix, g_pre_ffn,
           g_post_ffn, w_in, sgu_ln_g, sgu_ln_b, sgu_w, sgu_b, na_rpb, w_pa, w_pb, w_out, w_ff1, w_ff2):
    x = jnp.concatenate([x_prompt.reshape(N_CTX_TOK, D_MODEL), x_sample.reshape(DEC_BATCH * DEC_SEQ, D_MODEL)], axis=0)
    cvec = jnp.concatenate([c_ctx[None, :], c, jnp.zeros((MOD_ROWS - 1 - DEC_BATCH, D_MODEL), _f32)], axis=0)
    mod = _modulation(cvec, w_mod, b_mod).reshape(DEPTH, MOD_ROWS, 1, N_MOD * D_MODEL)
    bias = _na_bias_tables(na_rpb)
    cos_t, sin_t = _rope_tables()
    gdim = A_WIDTH // A_GROUPS
    sgu_b_lanes = jnp.broadcast_to(sgu_b[:, :, :, None], (DEPTH, A_GROUPS, CHUNK, gdim))

    ks_out, vs_out = [], []
    for l in range(DEPTH):
        row = lambda a: a[l][None, :]
        proj = _in_proj(x, mod[l], row(g_pre_mix), w_in[l].astype(_bf16), cos_t, sin_t)
        ya = _sgu(proj, row(sgu_ln_g), row(sgu_ln_b), sgu_w[l].astype(_bf16), sgu_b_lanes[l])
        yb = jnp.concatenate([_ctx_attention(proj), _na_attention(proj, cache_ctx_k, cache_ctx_v, bias, l)], axis=0)
        x = _mix(proj, ya, yb, w_pa[l].astype(_bf16), w_pb[l].astype(_bf16), w_out[l].astype(_bf16), x, mod[l],
                 row(g_post_mix))
        x = _ffn(x, mod[l], row(g_pre_ffn), row(g_post_ffn), w_ff1[l].astype(_bf16), w_ff2[l].astype(_bf16))
        kv = proj[:N_CTX_TOK, 2 * A_WIDTH + NA_WIDTH:2 * A_WIDTH + 3 * NA_WIDTH]
        kv = kv.reshape(BATCH, SEQ, 2, NA_HEADS, NA_HEAD_DIM).transpose(2, 0, 3, 1, 4)
        ks_out.append(kv[0])
        vs_out.append(kv[1])
    y_prompt = x[:N_CTX_TOK].reshape(BATCH, SEQ, D_MODEL)
    y_sample = x[N_CTX_TOK:].reshape(DEC_BATCH, DEC_SEQ, D_MODEL)
    return (y_prompt, y_sample, jnp.stack(ks_out, axis=1), jnp.stack(vs_out, axis=1))
```

```python
import functools

import numpy as np
import jax
import jax.numpy as jnp
from jax import lax
from jax.experimental import pallas as pl
from jax.experimental.pallas import tpu as pltpu

D_MODEL = 2048
BATCH = 32
SEQ = 256
DEPTH = 4
DEC_BATCH = 2
DEC_SEQ = 1024
PAST_LEN = 512
GRID_W = 64
CHUNK = 128
A_WIDTH = 1024
A_GROUPS = 8
NA_HEADS = 16
NA_HEAD_DIM = 64
NA_WIDTH = NA_HEADS * NA_HEAD_DIM
NA_KH_MAX = 8
NA_KW = 16
D_FF = 4 * D_MODEL
ROPE_THETA = 10000.0
EPS = 1e-6
N_MOD = 6
IN_COLS = 2 * A_WIDTH + 3 * NA_WIDTH + 2 * D_MODEL

N_CTX_TOK = BATCH * SEQ
N_TOK = N_CTX_TOK + DEC_BATCH * DEC_SEQ
ROWS = DEC_SEQ // GRID_W
NA_KH = min(NA_KH_MAX, ROWS)
MOD_ROWS = 8
LANES = 128
HEADS_PER_BLOCK = LANES // NA_HEAD_DIM
N_HEAD_BLOCKS = NA_HEADS // HEADS_PER_BLOCK

NA_QROWS = 2
NA_QBLK = NA_QROWS * GRID_W
NA_NPAIR = ROWS // NA_QROWS
NA_TOEP_ROWS = 2 * NA_KH_MAX - 1

VMEM_LIMIT = 56 * 1024 * 1024

_f32 = jnp.float32
_bf16 = jnp.bfloat16
_NT = (((1,), (1,)), ((), ()))


def _na_window_plan():
    plan = []
    for p in range(NA_NPAIR):
        rows = [NA_QROWS * p + rr for rr in range(NA_QROWS)]
        starts = [int(np.clip(r - NA_KH // 2, 0, ROWS - NA_KH)) for r in rows]
        w0 = min(starts) // 2 * 2
        w1 = -(-(max(starts) + NA_KH) // 2) * 2
        per_row = []
        for r, rs in zip(rows, starts):
            blocks = []
            for i in range(w0, w1, 2):
                keep_lo = rs <= i < rs + NA_KH
                keep_hi = rs <= i + 1 < rs + NA_KH
                a = i - r + NA_KH_MAX - 1
                assert not (keep_lo or keep_hi) or 0 <= a <= NA_TOEP_ROWS - 2
                blocks.append((a, keep_lo, keep_hi))
            per_row.append(blocks)
        plan.append((w0, w1 - w0, per_row))
    return plan


NA_PLAN = _na_window_plan()


def _mod_row(tile, tm):
    return jnp.maximum((tile * tm - N_CTX_TOK) // DEC_SEQ + 1, 0)


def _sigmoid(x):
    return 1.0 / (1.0 + jnp.exp(-x))


def _gelu(x):
    return 0.5 * x * (1.0 + jnp.tanh(np.sqrt(2.0 / np.pi).astype(np.float32) * (x + 0.044715 * (x * x * x))))


def _modnorm(x, g, scale, shift):
    y = x * lax.rsqrt(jnp.mean(x * x, axis=-1, keepdims=True) + EPS)
    return (y * g) * (1.0 + scale) + shift


def _rmsnorm(x, g):
    return (x * lax.rsqrt(jnp.mean(x * x, axis=-1, keepdims=True) + EPS)) * g


def _mod_kernel(c_ref, w_ref, b_ref, o_ref):
    c = c_ref[...]
    s = (c * _sigmoid(c)).astype(_bf16)
    o_ref[0] = jnp.dot(s, w_ref[0].astype(_bf16), preferred_element_type=_f32) + b_ref[0]


def _modulation(cvec, w_mod, b_mod):
    tn = 1024
    n_cols = N_MOD * D_MODEL
    return pl.pallas_call(
        _mod_kernel,
        out_shape=jax.ShapeDtypeStruct((DEPTH, MOD_ROWS, n_cols), _f32),
        grid=(DEPTH, n_cols // tn),
        in_specs=[
            pl.BlockSpec((MOD_ROWS, D_MODEL), lambda l, j: (0, 0)),
            pl.BlockSpec((1, D_MODEL, tn), lambda l, j: (l, 0, j)),
            pl.BlockSpec((1, 1, tn), lambda l, j: (l, 0, j)),
        ],
        out_specs=pl.BlockSpec((1, MOD_ROWS, tn), lambda l, j: (l, 0, j)),
        compiler_params=pltpu.CompilerParams(
            dimension_semantics=("arbitrary", "arbitrary"), vmem_limit_bytes=VMEM_LIMIT),
        name="modulation",
    )(cvec, w_mod, b_mod.reshape(DEPTH, 1, n_cols))


def _toeplitz_kernel(rpb_ref, onehot_ref, mask_ref, o_ref):
    o_ref[...] = jnp.dot(rpb_ref[...], onehot_ref[...], preferred_element_type=_f32,
                         precision=lax.Precision.HIGHEST) + mask_ref[...]


def _na_bias_tables(na_rpb):
    n_row = NA_TOEP_ROWS
    n_col = 2 * NA_KW - 1
    n_col_pad = 32
    cols = np.arange(GRID_W)
    cstart = np.clip(cols - NA_KW // 2, 0, GRID_W - NA_KW)
    in_win = (cols[None, :] >= cstart[:, None]) & (cols[None, :] < cstart[:, None] + NA_KW)
    col_idx = np.clip(cols[None, :] - cols[:, None] + NA_KW - 1, 0, n_col - 1)
    onehot = (col_idx.reshape(1, -1) == np.arange(n_col_pad)[:, None]) & in_win.reshape(1, -1)
    mask = np.where(in_win.reshape(1, -1), 0.0, -np.inf).astype(np.float32)
    rpb2 = jnp.pad(na_rpb.reshape(DEPTH * NA_HEADS * n_row, n_col), ((0, 0), (0, n_col_pad - n_col)))
    toep = pl.pallas_call(
        _toeplitz_kernel,
        out_shape=jax.ShapeDtypeStruct((DEPTH * NA_HEADS * n_row, GRID_W * GRID_W), _f32),
        compiler_params=pltpu.CompilerParams(vmem_limit_bytes=VMEM_LIMIT),
        name="rpb_toeplitz",
    )(rpb2, jnp.asarray(onehot.astype(np.float32)), jnp.asarray(mask))
    toep = toep.reshape(DEPTH, NA_HEADS, n_row, GRID_W, GRID_W)
    pairs = jnp.concatenate([toep[:, :, :-1], toep[:, :, 1:]], axis=-1)
    return pairs.reshape(DEPTH, N_HEAD_BLOCKS, HEADS_PER_BLOCK, n_row - 1, GRID_W, 2 * GRID_W)


def _rope_tables():
    t = jnp.arange(DEC_SEQ)
    half = NA_HEAD_DIM // 4
    freqs = ROPE_THETA ** (-jnp.arange(half, dtype=_f32) / half)
    ang_r = (t // GRID_W).astype(_f32)[:, None] * freqs[None, :]
    ang_c = (t % GRID_W).astype(_f32)[:, None] * freqs[None, :]
    cos_h = jnp.concatenate([jnp.cos(ang_r)] * 2 + [jnp.cos(ang_c)] * 2, axis=-1)
    sin_h = jnp.concatenate([-jnp.sin(ang_r), jnp.sin(ang_r), -jnp.sin(ang_c), jnp.sin(ang_c)], axis=-1)
    return jnp.tile(cos_h, (1, HEADS_PER_BLOCK)), jnp.tile(sin_h, (1, HEADS_PER_BLOCK))


IN_TM = 1024
IN_TN = 1024
IN_PROJ_COLS = 2 * A_WIDTH + 3 * NA_WIDTH + 2 * D_MODEL
Q_COL_TILE = (2 * A_WIDTH) // IN_TN
K_COL_TILE = (2 * A_WIDTH + NA_WIDTH) // IN_TN
ROW_CHUNK = 256


def _in_kernel(x_ref, shift_ref, scale_ref, g_ref, w_ref, cos_ref, sin_ref, o_ref, h_ref):
    i = pl.program_id(0)
    j = pl.program_id(1)

    @pl.when(j == 0)
    def _():
        def body(c, carry):
            rows = pl.ds(pl.multiple_of(c * ROW_CHUNK, ROW_CHUNK), ROW_CHUNK)
            h = _modnorm(x_ref[rows, :], g_ref[...], scale_ref[0], shift_ref[0])
            h_ref[rows, :] = h.astype(_bf16)
            return carry
        lax.fori_loop(0, IN_TM // ROW_CHUNK, body, 0)

    o_ref[...] = jnp.dot(h_ref[...], w_ref[...], preferred_element_type=_f32)
    is_rope = jnp.logical_and(i >= N_CTX_TOK // IN_TM, jnp.logical_or(j == Q_COL_TILE, j == K_COL_TILE))

    @pl.when(is_rope)
    def _():
        lane = lax.broadcasted_iota(jnp.int32, (IN_TM, LANES), 1)
        first_half = (lane % (NA_HEAD_DIM // 2)) < (NA_HEAD_DIM // 4)
        cos = cos_ref[...]
        sin = sin_ref[...]
        for cb in range(IN_TN // LANES):
            cols = slice(cb * LANES, (cb + 1) * LANES)
            xb = o_ref[:, cols]
            partner = jnp.where(first_half, pltpu.roll(xb, LANES - NA_HEAD_DIM // 4, axis=1),
                                pltpu.roll(xb, NA_HEAD_DIM // 4, axis=1))
            o_ref[:, cols] = xb * cos + partner * sin


def _in_proj(x, mod_l, g, w_in, cos_t, sin_t):
    n_m = N_TOK // IN_TM
    mod_spec = lambda chunk: pl.BlockSpec((1, 1, D_MODEL), lambda i, j: (_mod_row(i, IN_TM), 0, chunk))
    return pl.pallas_call(
        _in_kernel,
        out_shape=jax.ShapeDtypeStruct((N_TOK, IN_PROJ_COLS), _f32),
        grid=(n_m, IN_PROJ_COLS // IN_TN),
        in_specs=[
            pl.BlockSpec((IN_TM, D_MODEL), lambda i, j: (i, 0)),
            mod_spec(0), mod_spec(1),
            pl.BlockSpec((1, D_MODEL), lambda i, j: (0, 0)),
            pl.BlockSpec((D_MODEL, IN_TN), lambda i, j: (0, j)),
            pl.BlockSpec((DEC_SEQ, LANES), lambda i, j: (0, 0)),
            pl.BlockSpec((DEC_SEQ, LANES), lambda i, j: (0, 0)),
        ],
        out_specs=pl.BlockSpec((IN_TM, IN_TN), lambda i, j: (i, j)),
        scratch_shapes=[pltpu.VMEM((IN_TM, D_MODEL), _bf16)],
        compiler_params=pltpu.CompilerParams(
            dimension_semantics=("arbitrary", "arbitrary"), vmem_limit_bytes=VMEM_LIMIT),
        name="in_proj",
    )(x, mod_l, mod_l, g, w_in, cos_t, sin_t)


SGU_TM = 1024


def _sgu_kernel(au_ref, av_ref, lng_ref, lnb_ref, ws_ref, bs_ref, o_ref):
    gdim = A_WIDTH // A_GROUPS

    def body(c, carry):
        rows = pl.ds(pl.multiple_of(c * CHUNK, CHUNK), CHUNK)
        u = _gelu(au_ref[rows, :])
        gv = _gelu(av_ref[rows, :])
        mu = jnp.mean(gv, axis=-1, keepdims=True)
        var = jnp.mean(jnp.square(gv - mu), axis=-1, keepdims=True)
        vn = ((gv - mu) * lax.rsqrt(var + EPS) * lng_ref[...] + lnb_ref[...]).astype(_bf16)
        for g in range(A_GROUPS):
            cols = slice(g * gdim, (g + 1) * gdim)
            s = jnp.dot(ws_ref[g], vn[:, cols], preferred_element_type=_f32) + bs_ref[g]
            o_ref[rows, cols] = (u[:, cols] * s).astype(_bf16)
        return carry

    lax.fori_loop(0, SGU_TM // CHUNK, body, 0)


def _sgu(proj, ln_g, ln_b, w_s, b_s):
    gdim = A_WIDTH // A_GROUPS
    return pl.pallas_call(
        _sgu_kernel,
        out_shape=jax.ShapeDtypeStruct((N_TOK, A_WIDTH), _bf16),
        grid=(N_TOK // SGU_TM,),
        in_specs=[
            pl.BlockSpec((SGU_TM, A_WIDTH), lambda i: (i, 0)),
            pl.BlockSpec((SGU_TM, A_WIDTH), lambda i: (i, 1)),
            pl.BlockSpec((1, A_WIDTH), lambda i: (0, 0)),
            pl.BlockSpec((1, A_WIDTH), lambda i: (0, 0)),
            pl.BlockSpec((A_GROUPS, CHUNK, CHUNK), lambda i: (0, 0, 0)),
            pl.BlockSpec((A_GROUPS, CHUNK, gdim), lambda i: (0, 0, 0)),
        ],
        out_specs=pl.BlockSpec((SGU_TM, A_WIDTH), lambda i: (i, 0)),
        compiler_params=pltpu.CompilerParams(dimension_semantics=("arbitrary",), vmem_limit_bytes=VMEM_LIMIT),
        name="sgu",
    )(proj, proj, ln_g, ln_b, w_s, b_s)


ATT_SCALE = NA_HEAD_DIM ** -0.5
Q_LANE_BLOCK = (2 * A_WIDTH) // LANES
K_LANE_BLOCK = (2 * A_WIDTH + NA_WIDTH) // LANES
V_LANE_BLOCK = (2 * A_WIDTH + 2 * NA_WIDTH) // LANES


def _softmax_pv(parts):
    m = functools.reduce(jnp.maximum, [jnp.max(s, axis=-1, keepdims=True) for s, _ in parts])
    es = [jnp.exp(s - m) for s, _ in parts]
    l = functools.reduce(jnp.add, [jnp.sum(e, axis=-1, keepdims=True) for e in es])
    o = functools.reduce(jnp.add, [jnp.dot(e.astype(_bf16), v, preferred_element_type=_f32)
                                   for e, (_, v) in zip(es, parts)])
    return o / l


CTX_SEQS_PER_STEP = 4
CTX_TM = CTX_SEQS_PER_STEP * SEQ


def _ctx_attn_kernel(q_ref, k_ref, v_ref, *rest):
    o_ref, kst_ref, vst_ref = rest[-3:]
    seq_outs = []
    for s in range(CTX_SEQS_PER_STEP):
        rows = slice(s * SEQ, (s + 1) * SEQ)
        outs = []
        for hh in range(HEADS_PER_BLOCK):
            cols = slice(hh * NA_HEAD_DIM, (hh + 1) * NA_HEAD_DIM)
            k32 = k_ref[rows, cols]
            v32 = v_ref[rows, cols]
            kst_ref[s, 0, hh] = k32
            vst_ref[s, 0, hh] = v32
            q = (q_ref[rows, cols] * ATT_SCALE).astype(_bf16)
            sc = lax.dot_general(q, k32.astype(_bf16), _NT, preferred_element_type=_f32)
            outs.append(_softmax_pv([(sc, v32.astype(_bf16))]))
        seq_outs.append(jnp.concatenate(outs, axis=-1))
    o_ref[...] = jnp.concatenate(seq_outs, axis=0).astype(_bf16)


def _ctx_attention(proj, layer, states):
    state_shape = jax.ShapeDtypeStruct((BATCH, DEPTH, NA_HEADS, SEQ, NA_HEAD_DIM), _f32)
    state_spec = pl.BlockSpec((CTX_SEQS_PER_STEP, 1, HEADS_PER_BLOCK, SEQ, NA_HEAD_DIM),
                              lambda i, h: (i, layer, h, 0, 0))
    n_state_in = len(states)
    return pl.pallas_call(
        _ctx_attn_kernel,
        out_shape=(jax.ShapeDtypeStruct((N_TOK, NA_WIDTH), _bf16), state_shape, state_shape),
        grid=(N_CTX_TOK // CTX_TM, N_HEAD_BLOCKS),
        in_specs=[
            pl.BlockSpec((CTX_TM, LANES), lambda i, h: (i, Q_LANE_BLOCK + h)),
            pl.BlockSpec((CTX_TM, LANES), lambda i, h: (i, K_LANE_BLOCK + h)),
            pl.BlockSpec((CTX_TM, LANES), lambda i, h: (i, V_LANE_BLOCK + h)),
        ] + [pl.BlockSpec(memory_space=pl.ANY)] * n_state_in,
        out_specs=(pl.BlockSpec((CTX_TM, LANES), lambda i, h: (i, h)), state_spec, state_spec),
        input_output_aliases={3 + n: 1 + n for n in range(n_state_in)},
        compiler_params=pltpu.CompilerParams(
            dimension_semantics=("arbitrary", "arbitrary"), vmem_limit_bytes=VMEM_LIMIT),
        name="ctx_attention",
    )(proj, proj, proj, *states)


def _na_pair_bias(bias_ref, hh, blocks, lane_lo):
    pieces = []
    for a, keep_lo, keep_hi in blocks:
        if not (keep_lo or keep_hi):
            pieces.append(jnp.full((GRID_W, 2 * GRID_W), -jnp.inf, _f32))
            continue
        piece = bias_ref[0, 0, hh, a]
        if not keep_lo:
            piece = jnp.where(lane_lo, -jnp.inf, piece)
        if not keep_hi:
            piece = jnp.where(lane_lo, piece, -jnp.inf)
        pieces.append(piece)
    return jnp.concatenate(pieces, axis=-1)


def _na_attn_kernel(q_ref, k_ref, v_ref, ck_ref, cv_ref, bias_ref, yb_hbm_ref, o_ref):
    del yb_hbm_ref
    lane_lo = lax.broadcasted_iota(jnp.int32, (GRID_W, 2 * GRID_W), 1) < GRID_W
    head_outs = []
    for hh in range(HEADS_PER_BLOCK):
        pair_outs = []
        cols = slice(hh * NA_HEAD_DIM, (hh + 1) * NA_HEAD_DIM)
        q = (q_ref[:, cols] * ATT_SCALE).astype(_bf16)
        k = k_ref[:, cols].astype(_bf16)
        v = v_ref[:, cols].astype(_bf16)
        ck = ck_ref[0, 0, hh].astype(_bf16)
        cv = cv_ref[0, 0, hh].astype(_bf16)
        s_ctx = lax.dot_general(q, ck, _NT, preferred_element_type=_f32)
        for p, (w0, n_rows, per_row) in enumerate(NA_PLAN):
            qrows = slice(p * NA_QBLK, (p + 1) * NA_QBLK)
            krows = slice(w0 * GRID_W, (w0 + n_rows) * GRID_W)
            bias = jnp.concatenate([_na_pair_bias(bias_ref, hh, blocks, lane_lo) for blocks in per_row], axis=0)
            s_win = lax.dot_general(q[qrows], k[krows], _NT, preferred_element_type=_f32) + bias
            pair_outs.append(_softmax_pv([(s_win, v[krows]), (s_ctx[qrows], cv)]))
        head_outs.append(jnp.concatenate(pair_outs, axis=0))
    o_ref[...] = jnp.concatenate(head_outs, axis=-1).astype(_bf16)


def _na_attention(proj, cache_k, cache_v, bias_tab, layer, yb):
    tile0 = N_CTX_TOK // DEC_SEQ
    qkv_spec = lambda lane_block: pl.BlockSpec((DEC_SEQ, LANES), lambda h, b: (tile0 + b, lane_block + h))
    cache_spec = pl.BlockSpec((1, 1, HEADS_PER_BLOCK, PAST_LEN, NA_HEAD_DIM), lambda h, b: (b, layer, h, 0, 0))
    return pl.pallas_call(
        _na_attn_kernel,
        out_shape=jax.ShapeDtypeStruct((N_TOK, NA_WIDTH), _bf16),
        grid=(N_HEAD_BLOCKS, DEC_BATCH),
        in_specs=[
            qkv_spec(Q_LANE_BLOCK), qkv_spec(K_LANE_BLOCK), qkv_spec(V_LANE_BLOCK),
            cache_spec, cache_spec,
            pl.BlockSpec((1, 1, HEADS_PER_BLOCK, NA_TOEP_ROWS - 1, GRID_W, 2 * GRID_W),
                         lambda h, b: (layer, h, 0, 0, 0, 0)),
            pl.BlockSpec(memory_space=pl.ANY),
        ],
        out_specs=pl.BlockSpec((DEC_SEQ, LANES), lambda h, b: (tile0 + b, h)),
        input_output_aliases={6: 0},
        compiler_params=pltpu.CompilerParams(
            dimension_semantics=("arbitrary", "arbitrary"), vmem_limit_bytes=VMEM_LIMIT),
        name="na_attention",
    )(proj, proj, proj, cache_k, cache_v, bias_tab, yb)


MIX_TM = 512
MIX_TC = 512
GA_COL_TILE = (2 * A_WIDTH + 3 * NA_WIDTH) // MIX_TC
GB_COL_TILE = GA_COL_TILE + D_MODEL // MIX_TC


def _mix_kernel(ga_ref, gb_ref, ya_ref, yb_ref, wpa_ref, wpb_ref, wout_ref, x_ref, gate_ref, g_ref, o_ref):
    j = pl.program_id(1)

    @pl.when(j == 0)
    def _():
        o_ref[...] = jnp.zeros_like(o_ref)

    pa = jnp.dot(ya_ref[...], wpa_ref[...], preferred_element_type=_f32)
    pb = jnp.dot(yb_ref[...], wpb_ref[...], preferred_element_type=_f32)
    merged = (_sigmoid(ga_ref[...]) * pa + _sigmoid(gb_ref[...]) * pb).astype(_bf16)
    o_ref[...] += jnp.dot(merged, wout_ref[...], preferred_element_type=_f32)

    @pl.when(j == pl.num_programs(1) - 1)
    def _():
        o_ref[...] = x_ref[...] + gate_ref[0] * _rmsnorm(o_ref[...], g_ref[...])


def _mix(proj, ya, yb, w_pa, w_pb, w_out, x, mod_l, g_post):
    return pl.pallas_call(
        _mix_kernel,
        out_shape=jax.ShapeDtypeStruct((N_TOK, D_MODEL), _f32),
        grid=(N_TOK // MIX_TM, D_MODEL // MIX_TC),
        in_specs=[
            pl.BlockSpec((MIX_TM, MIX_TC), lambda i, j: (i, GA_COL_TILE + j)),
            pl.BlockSpec((MIX_TM, MIX_TC), lambda i, j: (i, GB_COL_TILE + j)),
            pl.BlockSpec((MIX_TM, A_WIDTH), lambda i, j: (i, 0)),
            pl.BlockSpec((MIX_TM, NA_WIDTH), lambda i, j: (i, 0)),
            pl.BlockSpec((A_WIDTH, MIX_TC), lambda i, j: (0, j)),
            pl.BlockSpec((NA_WIDTH, MIX_TC), lambda i, j: (0, j)),
            pl.BlockSpec((MIX_TC, D_MODEL), lambda i, j: (j, 0)),
            pl.BlockSpec((MIX_TM, D_MODEL), lambda i, j: (i, 0)),
            pl.BlockSpec((1, 1, D_MODEL), lambda i, j: (_mod_row(i, MIX_TM), 0, 2)),
            pl.BlockSpec((1, D_MODEL), lambda i, j: (0, 0)),
        ],
        out_specs=pl.BlockSpec((MIX_TM, D_MODEL), lambda i, j: (i, 0)),
        compiler_params=pltpu.CompilerParams(
            dimension_semantics=("arbitrary", "arbitrary"), vmem_limit_bytes=VMEM_LIMIT),
        name="mix_out",
    )(proj, proj, ya, yb, w_pa, w_pb, w_out, x, mod_l, g_post)


FFN_TM = 1024
FFN_TF = 512


FFN_OUT_SPLIT = 2


def _ffn_kernel(x_ref, shift_ref, scale_ref, gate_ref, gpre_ref, gpost_ref, w1_ref, w2_ref, o_ref, h_ref):
    k = pl.program_id(1)

    @pl.when(k == 0)
    def _():
        def body(c, carry):
            rows = pl.ds(pl.multiple_of(c * ROW_CHUNK, ROW_CHUNK), ROW_CHUNK)
            h = _modnorm(x_ref[rows, :], gpre_ref[...], scale_ref[0], shift_ref[0])
            h_ref[rows, :] = h.astype(_bf16)
            o_ref[rows, :] = jnp.zeros((ROW_CHUNK, D_MODEL), _f32)
            return carry
        lax.fori_loop(0, FFN_TM // ROW_CHUNK, body, 0)

    a = jnp.dot(h_ref[...], w1_ref[...], preferred_element_type=_f32)
    a = jnp.square(jnp.maximum(a, 0.0)).astype(_bf16)
    width = D_MODEL // FFN_OUT_SPLIT
    for s in range(FFN_OUT_SPLIT):
        cols = slice(s * width, (s + 1) * width)
        o_ref[:, cols] += jnp.dot(a, w2_ref[:, cols], preferred_element_type=_f32)

    @pl.when(k == pl.num_programs(1) - 1)
    def _():
        def body(c, carry):
            rows = pl.ds(pl.multiple_of(c * ROW_CHUNK, ROW_CHUNK), ROW_CHUNK)
            o_ref[rows, :] = x_ref[rows, :] + gate_ref[0] * _rmsnorm(o_ref[rows, :], gpost_ref[...])
            return carry
        lax.fori_loop(0, FFN_TM // ROW_CHUNK, body, 0)


def _ffn(x, mod_l, g_pre, g_post, w1, w2):
    mod_spec = lambda chunk: pl.BlockSpec((1, 1, D_MODEL), lambda i, k: (_mod_row(i, FFN_TM), 0, chunk))
    return pl.pallas_call(
        _ffn_kernel,
        out_shape=jax.ShapeDtypeStruct((N_TOK, D_MODEL), _f32),
        grid=(N_TOK // FFN_TM, D_FF // FFN_TF),
        in_specs=[
            pl.BlockSpec((FFN_TM, D_MODEL), lambda i, k: (i, 0)),
            mod_spec(3), mod_spec(4), mod_spec(5),
            pl.BlockSpec((1, D_MODEL), lambda i, k: (0, 0)),
            pl.BlockSpec((1, D_MODEL), lambda i, k: (0, 0)),
            pl.BlockSpec((D_MODEL, FFN_TF), lambda i, k: (0, k)),
            pl.BlockSpec((FFN_TF, D_MODEL), lambda i, k: (k, 0)),
        ],
        out_specs=pl.BlockSpec((FFN_TM, D_MODEL), lambda i, k: (i, 0)),
        scratch_shapes=[pltpu.VMEM((FFN_TM, D_MODEL), _bf16)],
        compiler_params=pltpu.CompilerParams(
            dimension_semantics=("arbitrary", "arbitrary"), vmem_limit_bytes=VMEM_LIMIT),
        name="ffn",
    )(x, mod_l, mod_l, mod_l, g_pre, g_post, w1, w2)


def kernel(x_prompt, x_sample, cache_ctx_k, cache_ctx_v, c, c_ctx, w_mod, b_mod, g_pre_mix, g_post_mix, g_pre_ffn,
           g_post_ffn, w_in, sgu_ln_g, sgu_ln_b, sgu_w, sgu_b, na_rpb, w_pa, w_pb, w_out, w_ff1, w_ff2):
    x = jnp.concatenate([x_prompt.reshape(N_CTX_TOK, D_MODEL), x_sample.reshape(DEC_BATCH * DEC_SEQ, D_MODEL)], axis=0)
    cvec = jnp.concatenate([c_ctx[None, :], c, jnp.zeros((MOD_ROWS - 1 - DEC_BATCH, D_MODEL), _f32)], axis=0)
    mod = _modulation(cvec, w_mod, b_mod).reshape(DEPTH, MOD_ROWS, 1, N_MOD * D_MODEL)
    bias = _na_bias_tables(na_rpb)
    cos_t, sin_t = _rope_tables()
    gdim = A_WIDTH // A_GROUPS
    sgu_b_lanes = jnp.broadcast_to(sgu_b[:, :, :, None], (DEPTH, A_GROUPS, CHUNK, gdim))

    states = ()
    for l in range(DEPTH):
        row = lambda a: a[l][None, :]
        proj = _in_proj(x, mod[l], row(g_pre_mix), w_in[l].astype(_bf16), cos_t, sin_t)
        ya = _sgu(proj, row(sgu_ln_g), row(sgu_ln_b), sgu_w[l].astype(_bf16), sgu_b_lanes[l])
        yb, *states = _ctx_attention(proj, l, states)
        yb = _na_attention(proj, cache_ctx_k, cache_ctx_v, bias, l, yb)
        x = _mix(proj, ya, yb, w_pa[l].astype(_bf16), w_pb[l].astype(_bf16), w_out[l].astype(_bf16), x, mod[l],
                 row(g_post_mix))
        x = _ffn(x, mod[l], row(g_pre_ffn), row(g_post_ffn), w_ff1[l].astype(_bf16), w_ff2[l].astype(_bf16))
    y_prompt = x[:N_CTX_TOK].reshape(BATCH, SEQ, D_MODEL)
    y_sample = x[N_CTX_TOK:].reshape(DEC_BATCH, DEC_SEQ, D_MODEL)
    return (y_prompt, y_sample, states[0], states[1])
```

```python
import functools

import numpy as np
import jax
import jax.numpy as jnp
from jax import lax
from jax.experimental import pallas as pl
from jax.experimental.pallas import tpu as pltpu

D_MODEL = 2048
BATCH = 32
SEQ = 256
DEPTH = 4
DEC_BATCH = 2
DEC_SEQ = 1024
PAST_LEN = 512
GRID_W = 64
CHUNK = 128
A_WIDTH = 1024
A_GROUPS = 8
NA_HEADS = 16
NA_HEAD_DIM = 64
NA_WIDTH = NA_HEADS * NA_HEAD_DIM
NA_KH_MAX = 8
NA_KW = 16
D_FF = 4 * D_MODEL
ROPE_THETA = 10000.0
EPS = 1e-6
N_MOD = 6
IN_COLS = 2 * A_WIDTH + 3 * NA_WIDTH + 2 * D_MODEL

N_CTX_TOK = BATCH * SEQ
N_TOK = N_CTX_TOK + DEC_BATCH * DEC_SEQ
ROWS = DEC_SEQ // GRID_W
NA_KH = min(NA_KH_MAX, ROWS)
MOD_ROWS = 8
LANES = 128
HEADS_PER_BLOCK = LANES // NA_HEAD_DIM
N_HEAD_BLOCKS = NA_HEADS // HEADS_PER_BLOCK

NA_QROWS = 2
NA_QBLK = NA_QROWS * GRID_W
NA_NPAIR = ROWS // NA_QROWS
NA_TOEP_ROWS = 2 * NA_KH_MAX - 1

VMEM_LIMIT = 56 * 1024 * 1024

_f32 = jnp.float32
_bf16 = jnp.bfloat16
_NT = (((1,), (1,)), ((), ()))


def _na_window_plan():
    plan = []
    for p in range(NA_NPAIR):
        rows = [NA_QROWS * p + rr for rr in range(NA_QROWS)]
        starts = [int(np.clip(r - NA_KH // 2, 0, ROWS - NA_KH)) for r in rows]
        w0 = min(starts) // 2 * 2
        w1 = -(-(max(starts) + NA_KH) // 2) * 2
        per_row = []
        for r, rs in zip(rows, starts):
            blocks = []
            for i in range(w0, w1, 2):
                keep_lo = rs <= i < rs + NA_KH
                keep_hi = rs <= i + 1 < rs + NA_KH
                a = i - r + NA_KH_MAX - 1
                assert not (keep_lo or keep_hi) or 0 <= a <= NA_TOEP_ROWS - 2
                blocks.append((a, keep_lo, keep_hi))
            per_row.append(blocks)
        plan.append((w0, w1 - w0, per_row))
    return plan


NA_PLAN = _na_window_plan()


def _mod_row(tile, tm):
    return jnp.maximum((tile * tm - N_CTX_TOK) // DEC_SEQ + 1, 0)


def _sigmoid(x):
    return 1.0 / (1.0 + jnp.exp(-x))


def _gelu(x):
    return 0.5 * x * (1.0 + jnp.tanh(np.sqrt(2.0 / np.pi).astype(np.float32) * (x + 0.044715 * (x * x * x))))


def _modnorm(x, g, scale, shift):
    y = x * lax.rsqrt(jnp.mean(x * x, axis=-1, keepdims=True) + EPS)
    return (y * g) * (1.0 + scale) + shift


def _rmsnorm(x, g):
    return (x * lax.rsqrt(jnp.mean(x * x, axis=-1, keepdims=True) + EPS)) * g


def _mod_kernel(c_ref, w_ref, b_ref, o_ref):
    c = c_ref[...]
    s = (c * _sigmoid(c)).astype(_bf16)
    o_ref[0] = jnp.dot(s, w_ref[0].astype(_bf16), preferred_element_type=_f32) + b_ref[0]


def _modulation(cvec, w_mod, b_mod):
    tn = 1024
    n_cols = N_MOD * D_MODEL
    return pl.pallas_call(
        _mod_kernel,
        out_shape=jax.ShapeDtypeStruct((DEPTH, MOD_ROWS, n_cols), _f32),
        grid=(DEPTH, n_cols // tn),
        in_specs=[
            pl.BlockSpec((MOD_ROWS, D_MODEL), lambda l, j: (0, 0)),
            pl.BlockSpec((1, D_MODEL, tn), lambda l, j: (l, 0, j)),
            pl.BlockSpec((1, 1, tn), lambda l, j: (l, 0, j)),
        ],
        out_specs=pl.BlockSpec((1, MOD_ROWS, tn), lambda l, j: (l, 0, j)),
        compiler_params=pltpu.CompilerParams(
            dimension_semantics=("arbitrary", "arbitrary"), vmem_limit_bytes=VMEM_LIMIT),
        name="modulation",
    )(cvec, w_mod, b_mod.reshape(DEPTH, 1, n_cols))


def _toeplitz_kernel(rpb_ref, onehot_ref, mask_ref, o_ref):
    o_ref[...] = jnp.dot(rpb_ref[...], onehot_ref[...], preferred_element_type=_f32,
                         precision=lax.Precision.HIGHEST) + mask_ref[...]


def _na_bias_tables(na_rpb):
    n_row = NA_TOEP_ROWS
    n_col = 2 * NA_KW - 1
    n_col_pad = 32
    cols = np.arange(GRID_W)
    cstart = np.clip(cols - NA_KW // 2, 0, GRID_W - NA_KW)
    in_win = (cols[None, :] >= cstart[:, None]) & (cols[None, :] < cstart[:, None] + NA_KW)
    col_idx = np.clip(cols[None, :] - cols[:, None] + NA_KW - 1, 0, n_col - 1)
    onehot = (col_idx.reshape(1, -1) == np.arange(n_col_pad)[:, None]) & in_win.reshape(1, -1)
    mask = np.where(in_win.reshape(1, -1), 0.0, -np.inf).astype(np.float32)
    rpb2 = jnp.pad(na_rpb.reshape(DEPTH * NA_HEADS * n_row, n_col), ((0, 0), (0, n_col_pad - n_col)))
    toep = pl.pallas_call(
        _toeplitz_kernel,
        out_shape=jax.ShapeDtypeStruct((DEPTH * NA_HEADS * n_row, GRID_W * GRID_W), _f32),
        compiler_params=pltpu.CompilerParams(vmem_limit_bytes=VMEM_LIMIT),
        name="rpb_toeplitz",
    )(rpb2, jnp.asarray(onehot.astype(np.float32)), jnp.asarray(mask))
    toep = toep.reshape(DEPTH, NA_HEADS, n_row, GRID_W, GRID_W)
    pairs = jnp.concatenate([toep[:, :, :-1], toep[:, :, 1:]], axis=-1)
    return pairs.reshape(DEPTH, N_HEAD_BLOCKS, HEADS_PER_BLOCK, n_row - 1, GRID_W, 2 * GRID_W)


def _rope_tables():
    t = jnp.arange(DEC_SEQ)
    half = NA_HEAD_DIM // 4
    freqs = ROPE_THETA ** (-jnp.arange(half, dtype=_f32) / half)
    ang_r = (t // GRID_W).astype(_f32)[:, None] * freqs[None, :]
    ang_c = (t % GRID_W).astype(_f32)[:, None] * freqs[None, :]
    cos_h = jnp.concatenate([jnp.cos(ang_r)] * 2 + [jnp.cos(ang_c)] * 2, axis=-1)
    sin_h = jnp.concatenate([-jnp.sin(ang_r), jnp.sin(ang_r), -jnp.sin(ang_c), jnp.sin(ang_c)], axis=-1)
    return jnp.tile(cos_h, (1, HEADS_PER_BLOCK)), jnp.tile(sin_h, (1, HEADS_PER_BLOCK))


IN_TM = 1024
IN_TN = 1024
IN_PROJ_COLS = 2 * A_WIDTH + 3 * NA_WIDTH + 2 * D_MODEL
Q_COL_TILE = (2 * A_WIDTH) // IN_TN
K_COL_TILE = (2 * A_WIDTH + NA_WIDTH) // IN_TN
ROW_CHUNK = 256


def _in_kernel(x_ref, shift_ref, scale_ref, g_ref, w_ref, cos_ref, sin_ref, o_ref, h_ref):
    i = pl.program_id(0)
    j = pl.program_id(1)

    @pl.when(j == 0)
    def _():
        def body(c, carry):
            rows = pl.ds(pl.multiple_of(c * ROW_CHUNK, ROW_CHUNK), ROW_CHUNK)
            h = _modnorm(x_ref[rows, :], g_ref[0], scale_ref[0, 0], shift_ref[0, 0])
            h_ref[rows, :] = h.astype(_bf16)
            return carry
        lax.fori_loop(0, IN_TM // ROW_CHUNK, body, 0)

    o_ref[...] = jnp.dot(h_ref[...], w_ref[0].astype(_bf16), preferred_element_type=_f32)
    is_rope = jnp.logical_and(i >= N_CTX_TOK // IN_TM, jnp.logical_or(j == Q_COL_TILE, j == K_COL_TILE))

    @pl.when(is_rope)
    def _():
        lane = lax.broadcasted_iota(jnp.int32, (IN_TM, LANES), 1)
        first_half = (lane % (NA_HEAD_DIM // 2)) < (NA_HEAD_DIM // 4)
        cos = cos_ref[...]
        sin = sin_ref[...]
        for cb in range(IN_TN // LANES):
            cols = slice(cb * LANES, (cb + 1) * LANES)
            xb = o_ref[:, cols]
            partner = jnp.where(first_half, pltpu.roll(xb, LANES - NA_HEAD_DIM // 4, axis=1),
                                pltpu.roll(xb, NA_HEAD_DIM // 4, axis=1))
            o_ref[:, cols] = xb * cos + partner * sin


def _mod_spec(layer, tm, chunk):
    return pl.BlockSpec((1, 1, 1, D_MODEL), lambda i, j: (layer, _mod_row(i, tm), 0, chunk))


def _layer_vec_spec(layer, width):
    return pl.BlockSpec((1, 1, width), lambda *_: (layer, 0, 0))


def _in_proj(x, mod, g, w_in, cos_t, sin_t, layer):
    n_m = N_TOK // IN_TM
    return pl.pallas_call(
        _in_kernel,
        out_shape=jax.ShapeDtypeStruct((N_TOK, IN_PROJ_COLS), _f32),
        grid=(n_m, IN_PROJ_COLS // IN_TN),
        in_specs=[
            pl.BlockSpec((IN_TM, D_MODEL), lambda i, j: (i, 0)),
            _mod_spec(layer, IN_TM, 0), _mod_spec(layer, IN_TM, 1),
            _layer_vec_spec(layer, D_MODEL),
            pl.BlockSpec((1, D_MODEL, IN_TN), lambda i, j: (layer, 0, j)),
            pl.BlockSpec((DEC_SEQ, LANES), lambda i, j: (0, 0)),
            pl.BlockSpec((DEC_SEQ, LANES), lambda i, j: (0, 0)),
        ],
        out_specs=pl.BlockSpec((IN_TM, IN_TN), lambda i, j: (i, j)),
        scratch_shapes=[pltpu.VMEM((IN_TM, D_MODEL), _bf16)],
        compiler_params=pltpu.CompilerParams(
            dimension_semantics=("arbitrary", "arbitrary"), vmem_limit_bytes=VMEM_LIMIT),
        name="in_proj",
    )(x, mod, mod, g, w_in, cos_t, sin_t)


SGU_TM = 1024


def _sgu_kernel(au_ref, av_ref, lng_ref, lnb_ref, ws_ref, bs_ref, o_ref):
    gdim = A_WIDTH // A_GROUPS

    def body(c, carry):
        rows = pl.ds(pl.multiple_of(c * CHUNK, CHUNK), CHUNK)
        u = _gelu(au_ref[rows, :])
        gv = _gelu(av_ref[rows, :])
        mu = jnp.mean(gv, axis=-1, keepdims=True)
        var = jnp.mean(jnp.square(gv - mu), axis=-1, keepdims=True)
        vn = ((gv - mu) * lax.rsqrt(var + EPS) * lng_ref[0] + lnb_ref[0]).astype(_bf16)
        for g in range(A_GROUPS):
            cols = slice(g * gdim, (g + 1) * gdim)
            s = jnp.dot(ws_ref[0, g], vn[:, cols], preferred_element_type=_f32) + bs_ref[0, g]
            o_ref[rows, cols] = (u[:, cols] * s).astype(_bf16)
        return carry

    lax.fori_loop(0, SGU_TM // CHUNK, body, 0)


def _sgu(proj, ln_g, ln_b, w_s, b_s, layer):
    gdim = A_WIDTH // A_GROUPS
    return pl.pallas_call(
        _sgu_kernel,
        out_shape=jax.ShapeDtypeStruct((N_TOK, A_WIDTH), _bf16),
        grid=(N_TOK // SGU_TM,),
        in_specs=[
            pl.BlockSpec((SGU_TM, A_WIDTH), lambda i: (i, 0)),
            pl.BlockSpec((SGU_TM, A_WIDTH), lambda i: (i, 1)),
            _layer_vec_spec(layer, A_WIDTH),
            _layer_vec_spec(layer, A_WIDTH),
            pl.BlockSpec((1, A_GROUPS, CHUNK, CHUNK), lambda i: (layer, 0, 0, 0)),
            pl.BlockSpec((1, A_GROUPS, CHUNK, gdim), lambda i: (layer, 0, 0, 0)),
        ],
        out_specs=pl.BlockSpec((SGU_TM, A_WIDTH), lambda i: (i, 0)),
        compiler_params=pltpu.CompilerParams(dimension_semantics=("arbitrary",), vmem_limit_bytes=VMEM_LIMIT),
        name="sgu",
    )(proj, proj, ln_g, ln_b, w_s, b_s)


ATT_SCALE = NA_HEAD_DIM ** -0.5
Q_LANE_BLOCK = (2 * A_WIDTH) // LANES
K_LANE_BLOCK = (2 * A_WIDTH + NA_WIDTH) // LANES
V_LANE_BLOCK = (2 * A_WIDTH + 2 * NA_WIDTH) // LANES


def _softmax_pv(parts):
    m = functools.reduce(jnp.maximum, [jnp.max(s, axis=-1, keepdims=True) for s, _ in parts])
    es = [jnp.exp(s - m) for s, _ in parts]
    l = functools.reduce(jnp.add, [jnp.sum(e, axis=-1, keepdims=True) for e in es])
    o = functools.reduce(jnp.add, [jnp.dot(e.astype(_bf16), v, preferred_element_type=_f32)
                                   for e, (_, v) in zip(es, parts)])
    return o / l


CTX_SEQS_PER_STEP = 4
CTX_TM = CTX_SEQS_PER_STEP * SEQ


def _head_lane_masks(rows):
    lane = lax.broadcasted_iota(jnp.int32, (rows, LANES), 1)
    return [jnp.logical_and(lane >= hh * NA_HEAD_DIM, lane < (hh + 1) * NA_HEAD_DIM)
            for hh in range(HEADS_PER_BLOCK)]


def _select_heads(masks, per_head):
    out = per_head[-1]
    for mask, val in zip(masks[:-1], per_head[:-1]):
        out = jnp.where(mask, val, out)
    return out


def _ctx_attn_kernel(q_ref, k_ref, v_ref, *rest):
    o_ref, kst_ref, vst_ref = rest[-3:]
    masks = _head_lane_masks(SEQ)
    seq_outs = []
    for s in range(CTX_SEQS_PER_STEP):
        rows = slice(s * SEQ, (s + 1) * SEQ)
        q32 = q_ref[rows, :] * ATT_SCALE
        k32 = k_ref[rows, :]
        v32 = v_ref[rows, :]
        for hh in range(HEADS_PER_BLOCK):
            cols = slice(hh * NA_HEAD_DIM, (hh + 1) * NA_HEAD_DIM)
            kst_ref[s, 0, hh] = k32[:, cols]
            vst_ref[s, 0, hh] = v32[:, cols]
        kb = k32.astype(_bf16)
        vb = v32.astype(_bf16)
        per_head = []
        for mask in masks:
            qh = jnp.where(mask, q32, 0.0).astype(_bf16)
            sc = lax.dot_general(qh, kb, _NT, preferred_element_type=_f32)
            per_head.append(_softmax_pv([(sc, vb)]))
        seq_outs.append(_select_heads(masks, per_head))
    o_ref[...] = jnp.concatenate(seq_outs, axis=0).astype(_bf16)


def _ctx_attention(proj, layer, states):
    state_shape = jax.ShapeDtypeStruct((BATCH, DEPTH, NA_HEADS, SEQ, NA_HEAD_DIM), _f32)
    state_spec = pl.BlockSpec((CTX_SEQS_PER_STEP, 1, HEADS_PER_BLOCK, SEQ, NA_HEAD_DIM),
                              lambda i, h: (i, layer, h, 0, 0))
    n_state_in = len(states)
    return pl.pallas_call(
        _ctx_attn_kernel,
        out_shape=(jax.ShapeDtypeStruct((N_TOK, NA_WIDTH), _bf16), state_shape, state_shape),
        grid=(N_CTX_TOK // CTX_TM, N_HEAD_BLOCKS),
        in_specs=[
            pl.BlockSpec((CTX_TM, LANES), lambda i, h: (i, Q_LANE_BLOCK + h)),
            pl.BlockSpec((CTX_TM, LANES), lambda i, h: (i, K_LANE_BLOCK + h)),
            pl.BlockSpec((CTX_TM, LANES), lambda i, h: (i, V_LANE_BLOCK + h)),
        ] + [pl.BlockSpec(memory_space=pl.ANY)] * n_state_in,
        out_specs=(pl.BlockSpec((CTX_TM, LANES), lambda i, h: (i, h)), state_spec, state_spec),
        input_output_aliases={3 + n: 1 + n for n in range(n_state_in)},
        compiler_params=pltpu.CompilerParams(
            dimension_semantics=("arbitrary", "arbitrary"), vmem_limit_bytes=VMEM_LIMIT),
        name="ctx_attention",
    )(proj, proj, proj, *states)


def _na_pair_bias(bias_ref, hh, blocks, lane_lo):
    pieces = []
    for a, keep_lo, keep_hi in blocks:
        if not (keep_lo or keep_hi):
            pieces.append(jnp.full((GRID_W, 2 * GRID_W), -jnp.inf, _f32))
            continue
        piece = bias_ref[0, 0, hh, a]
        if not keep_lo:
            piece = jnp.where(lane_lo, -jnp.inf, piece)
        if not keep_hi:
            piece = jnp.where(lane_lo, piece, -jnp.inf)
        pieces.append(piece)
    return jnp.concatenate(pieces, axis=-1)


def _na_attn_kernel(q_ref, k_ref, v_ref, ck_ref, cv_ref, bias_ref, yb_hbm_ref, o_ref):
    del yb_hbm_ref
    lane_lo = lax.broadcasted_iota(jnp.int32, (GRID_W, 2 * GRID_W), 1) < GRID_W
    masks = _head_lane_masks(DEC_SEQ)
    q32 = q_ref[...] * ATT_SCALE
    k = k_ref[...].astype(_bf16)
    v = v_ref[...].astype(_bf16)
    ck = jnp.concatenate([ck_ref[0, 0, hh] for hh in range(HEADS_PER_BLOCK)], axis=-1).astype(_bf16)
    cv = jnp.concatenate([cv_ref[0, 0, hh] for hh in range(HEADS_PER_BLOCK)], axis=-1).astype(_bf16)
    head_outs = []
    for hh, mask in enumerate(masks):
        pair_outs = []
        q = jnp.where(mask, q32, 0.0).astype(_bf16)
        s_ctx = lax.dot_general(q, ck, _NT, preferred_element_type=_f32)
        for p, (w0, n_rows, per_row) in enumerate(NA_PLAN):
            qrows = slice(p * NA_QBLK, (p + 1) * NA_QBLK)
            krows = slice(w0 * GRID_W, (w0 + n_rows) * GRID_W)
            bias = jnp.concatenate([_na_pair_bias(bias_ref, hh, blocks, lane_lo) for blocks in per_row], axis=0)
            s_win = lax.dot_general(q[qrows], k[krows], _NT, preferred_element_type=_f32) + bias
            pair_outs.append(_softmax_pv([(s_win, v[krows]), (s_ctx[qrows], cv)]))
        head_outs.append(jnp.concatenate(pair_outs, axis=0))
    o_ref[...] = _select_heads(masks, head_outs).astype(_bf16)


def _na_attention(proj, cache_k, cache_v, bias_tab, layer, yb):
    tile0 = N_CTX_TOK // DEC_SEQ
    qkv_spec = lambda lane_block: pl.BlockSpec((DEC_SEQ, LANES), lambda h, b: (tile0 + b, lane_block + h))
    cache_spec = pl.BlockSpec((1, 1, HEADS_PER_BLOCK, PAST_LEN, NA_HEAD_DIM), lambda h, b: (b, layer, h, 0, 0))
    return pl.pallas_call(
        _na_attn_kernel,
        out_shape=jax.ShapeDtypeStruct((N_TOK, NA_WIDTH), _bf16),
        grid=(N_HEAD_BLOCKS, DEC_BATCH),
        in_specs=[
            qkv_spec(Q_LANE_BLOCK), qkv_spec(K_LANE_BLOCK), qkv_spec(V_LANE_BLOCK),
            cache_spec, cache_spec,
            pl.BlockSpec((1, 1, HEADS_PER_BLOCK, NA_TOEP_ROWS - 1, GRID_W, 2 * GRID_W),
                         lambda h, b: (layer, h, 0, 0, 0, 0)),
            pl.BlockSpec(memory_space=pl.ANY),
        ],
        out_specs=pl.BlockSpec((DEC_SEQ, LANES), lambda h, b: (tile0 + b, h)),
        input_output_aliases={6: 0},
        compiler_params=pltpu.CompilerParams(
            dimension_semantics=("arbitrary", "arbitrary"), vmem_limit_bytes=VMEM_LIMIT),
        name="na_attention",
    )(proj, proj, proj, cache_k, cache_v, bias_tab, yb)


MIX_TM = 512
MIX_TC = 512
GA_COL_TILE = (2 * A_WIDTH + 3 * NA_WIDTH) // MIX_TC
GB_COL_TILE = GA_COL_TILE + D_MODEL // MIX_TC


def _mix_kernel(ga_ref, gb_ref, ya_ref, yb_ref, wpa_ref, wpb_ref, wout_ref, x_ref, gate_ref, g_ref, o_ref):
    j = pl.program_id(1)

    @pl.when(j == 0)
    def _():
        o_ref[...] = jnp.zeros_like(o_ref)

    pa = jnp.dot(ya_ref[...], wpa_ref[0], preferred_element_type=_f32)
    pb = jnp.dot(yb_ref[...], wpb_ref[0], preferred_element_type=_f32)
    merged = (_sigmoid(ga_ref[...]) * pa + _sigmoid(gb_ref[...]) * pb).astype(_bf16)
    o_ref[...] += jnp.dot(merged, wout_ref[0], preferred_element_type=_f32)

    @pl.when(j == pl.num_programs(1) - 1)
    def _():
        o_ref[...] = x_ref[...] + gate_ref[0, 0] * _rmsnorm(o_ref[...], g_ref[0])


def _mix(proj, ya, yb, w_pa, w_pb, w_out, x, mod, g_post, layer):
    return pl.pallas_call(
        _mix_kernel,
        out_shape=jax.ShapeDtypeStruct((N_TOK, D_MODEL), _f32),
        grid=(N_TOK // MIX_TM, D_MODEL // MIX_TC),
        in_specs=[
            pl.BlockSpec((MIX_TM, MIX_TC), lambda i, j: (i, GA_COL_TILE + j)),
            pl.BlockSpec((MIX_TM, MIX_TC), lambda i, j: (i, GB_COL_TILE + j)),
            pl.BlockSpec((MIX_TM, A_WIDTH), lambda i, j: (i, 0)),
            pl.BlockSpec((MIX_TM, NA_WIDTH), lambda i, j: (i, 0)),
            pl.BlockSpec((1, A_WIDTH, MIX_TC), lambda i, j: (layer, 0, j)),
            pl.BlockSpec((1, NA_WIDTH, MIX_TC), lambda i, j: (layer, 0, j)),
            pl.BlockSpec((1, MIX_TC, D_MODEL), lambda i, j: (layer, j, 0)),
            pl.BlockSpec((MIX_TM, D_MODEL), lambda i, j: (i, 0)),
            _mod_spec(layer, MIX_TM, 2),
            _layer_vec_spec(layer, D_MODEL),
        ],
        out_specs=pl.BlockSpec((MIX_TM, D_MODEL), lambda i, j: (i, 0)),
        compiler_params=pltpu.CompilerParams(
            dimension_semantics=("arbitrary", "arbitrary"), vmem_limit_bytes=VMEM_LIMIT),
        name="mix_out",
    )(proj, proj, ya, yb, w_pa, w_pb, w_out, x, mod, g_post)


FFN_TM = 1024
FFN_TF = 512


FFN_OUT_SPLIT = 2


def _ffn_kernel(x_ref, shift_ref, scale_ref, gate_ref, gpre_ref, gpost_ref, w1_ref, w2_ref, o_ref, h_ref):
    k = pl.program_id(1)

    @pl.when(k == 0)
    def _():
        def body(c, carry):
            rows = pl.ds(pl.multiple_of(c * ROW_CHUNK, ROW_CHUNK), ROW_CHUNK)
            h = _modnorm(x_ref[rows, :], gpre_ref[0], scale_ref[0, 0], shift_ref[0, 0])
            h_ref[rows, :] = h.astype(_bf16)
            o_ref[rows, :] = jnp.zeros((ROW_CHUNK, D_MODEL), _f32)
            return carry
        lax.fori_loop(0, FFN_TM // ROW_CHUNK, body, 0)

    a = jnp.dot(h_ref[...], w1_ref[0], preferred_element_type=_f32)
    a = jnp.square(jnp.maximum(a, 0.0)).astype(_bf16)
    width = D_MODEL // FFN_OUT_SPLIT
    for s in range(FFN_OUT_SPLIT):
        cols = slice(s * width, (s + 1) * width)
        o_ref[:, cols] += jnp.dot(a, w2_ref[0, :, cols], preferred_element_type=_f32)

    @pl.when(k == pl.num_programs(1) - 1)
    def _():
        def body(c, carry):
            rows = pl.ds(pl.multiple_of(c * ROW_CHUNK, ROW_CHUNK), ROW_CHUNK)
            o_ref[rows, :] = x_ref[rows, :] + gate_ref[0, 0] * _rmsnorm(o_ref[rows, :], gpost_ref[0])
            return carry
        lax.fori_loop(0, FFN_TM // ROW_CHUNK, body, 0)


def _ffn(x, mod, g_pre, g_post, w1, w2, layer):
    return pl.pallas_call(
        _ffn_kernel,
        out_shape=jax.ShapeDtypeStruct((N_TOK, D_MODEL), _f32),
        grid=(N_TOK // FFN_TM, D_FF // FFN_TF),
        in_specs=[
            pl.BlockSpec((FFN_TM, D_MODEL), lambda i, k: (i, 0)),
            _mod_spec(layer, FFN_TM, 3), _mod_spec(layer, FFN_TM, 4), _mod_spec(layer, FFN_TM, 5),
            _layer_vec_spec(layer, D_MODEL),
            _layer_vec_spec(layer, D_MODEL),
            pl.BlockSpec((1, D_MODEL, FFN_TF), lambda i, k: (layer, 0, k)),
            pl.BlockSpec((1, FFN_TF, D_MODEL), lambda i, k: (layer, k, 0)),
        ],
        out_specs=pl.BlockSpec((FFN_TM, D_MODEL), lambda i, k: (i, 0)),
        scratch_shapes=[pltpu.VMEM((FFN_TM, D_MODEL), _bf16)],
        compiler_params=pltpu.CompilerParams(
            dimension_semantics=("arbitrary", "arbitrary"), vmem_limit_bytes=VMEM_LIMIT),
        name="ffn",
    )(x, mod, mod, mod, g_pre, g_post, w1, w2)


def kernel(x_prompt, x_sample, cache_ctx_k, cache_ctx_v, c, c_ctx, w_mod, b_mod, g_pre_mix, g_post_mix, g_pre_ffn,
           g_post_ffn, w_in, sgu_ln_g, sgu_ln_b, sgu_w, sgu_b, na_rpb, w_pa, w_pb, w_out, w_ff1, w_ff2):
    x = jnp.concatenate([x_prompt.reshape(N_CTX_TOK, D_MODEL), x_sample.reshape(DEC_BATCH * DEC_SEQ, D_MODEL)], axis=0)
    cvec = jnp.concatenate([c_ctx[None, :], c, jnp.zeros((MOD_ROWS - 1 - DEC_BATCH, D_MODEL), _f32)], axis=0)
    mod = _modulation(cvec, w_mod, b_mod).reshape(DEPTH, MOD_ROWS, 1, N_MOD * D_MODEL)
    bias = _na_bias_tables(na_rpb)
    cos_t, sin_t = _rope_tables()
    gdim = A_WIDTH // A_GROUPS
    sgu_b_lanes = jnp.broadcast_to(sgu_b[:, :, :, None], (DEPTH, A_GROUPS, CHUNK, gdim))

    vec = lambda a: a[:, None, :]
    g_pre_mix, g_post_mix, g_pre_ffn, g_post_ffn = vec(g_pre_mix), vec(g_post_mix), vec(g_pre_ffn), vec(g_post_ffn)
    sgu_ln_g, sgu_ln_b = vec(sgu_ln_g), vec(sgu_ln_b)
    sgu_w, w_pa, w_pb, w_out, w_ff1, w_ff2 = (w.astype(_bf16) for w in (sgu_w, w_pa, w_pb, w_out, w_ff1, w_ff2))

    states = ()
    for l in range(DEPTH):
        proj = _in_proj(x, mod, g_pre_mix, w_in, cos_t, sin_t, l)
        ya = _sgu(proj, sgu_ln_g, sgu_ln_b, sgu_w, sgu_b_lanes, l)
        yb, *states = _ctx_attention(proj, l, states)
        yb = _na_attention(proj, cache_ctx_k, cache_ctx_v, bias, l, yb)
        x = _mix(proj, ya, yb, w_pa, w_pb, w_out, x, mod, g_post_mix, l)
        x = _ffn(x, mod, g_pre_ffn, g_post_ffn, w_ff1, w_ff2, l)
    y_prompt = x[:N_CTX_TOK].reshape(BATCH, SEQ, D_MODEL)
    y_sample = x[N_CTX_TOK:].reshape(DEC_BATCH, DEC_SEQ, D_MODEL)
    return (y_prompt, y_sample, states[0], states[1])
```

```python
import functools

import numpy as np
import jax
import jax.numpy as jnp
from jax import lax
from jax.experimental import pallas as pl
from jax.experimental.pallas import tpu as pltpu

D_MODEL = 2048
BATCH = 32
SEQ = 256
DEPTH = 4
DEC_BATCH = 2
DEC_SEQ = 1024
PAST_LEN = 512
GRID_W = 64
CHUNK = 128
A_WIDTH = 1024
A_GROUPS = 8
NA_HEADS = 16
NA_HEAD_DIM = 64
NA_WIDTH = NA_HEADS * NA_HEAD_DIM
NA_KH_MAX = 8
NA_KW = 16
D_FF = 4 * D_MODEL
ROPE_THETA = 10000.0
EPS = 1e-6
N_MOD = 6
IN_COLS = 2 * A_WIDTH + 3 * NA_WIDTH + 2 * D_MODEL

N_CTX_TOK = BATCH * SEQ
N_TOK = N_CTX_TOK + DEC_BATCH * DEC_SEQ
ROWS = DEC_SEQ // GRID_W
NA_KH = min(NA_KH_MAX, ROWS)
MOD_ROWS = 8
LANES = 128
HEADS_PER_BLOCK = LANES // NA_HEAD_DIM
N_HEAD_BLOCKS = NA_HEADS // HEADS_PER_BLOCK

NA_QROWS = 2
NA_QBLK = NA_QROWS * GRID_W
NA_NPAIR = ROWS // NA_QROWS
NA_TOEP_ROWS = 2 * NA_KH_MAX - 1

VMEM_LIMIT = 56 * 1024 * 1024

_f32 = jnp.float32
_bf16 = jnp.bfloat16
_NT = (((1,), (1,)), ((), ()))


def _na_window_plan():
    plan = []
    for p in range(NA_NPAIR):
        rows = [NA_QROWS * p + rr for rr in range(NA_QROWS)]
        starts = [int(np.clip(r - NA_KH // 2, 0, ROWS - NA_KH)) for r in rows]
        w0 = min(starts) // 2 * 2
        w1 = -(-(max(starts) + NA_KH) // 2) * 2
        per_row = []
        for r, rs in zip(rows, starts):
            blocks = []
            for i in range(w0, w1, 2):
                keep_lo = rs <= i < rs + NA_KH
                keep_hi = rs <= i + 1 < rs + NA_KH
                a = i - r + NA_KH_MAX - 1
                assert not (keep_lo or keep_hi) or 0 <= a <= NA_TOEP_ROWS - 2
                blocks.append((a, keep_lo, keep_hi))
            per_row.append(blocks)
        plan.append((w0, w1 - w0, per_row))
    return plan


NA_PLAN = _na_window_plan()


def _mod_row(tile, tm):
    return jnp.maximum((tile * tm - N_CTX_TOK) // DEC_SEQ + 1, 0)


def _sigmoid(x):
    return 1.0 / (1.0 + jnp.exp(-x))


def _gelu(x):
    return 0.5 * x * (1.0 + jnp.tanh(np.sqrt(2.0 / np.pi).astype(np.float32) * (x + 0.044715 * (x * x * x))))


def _modnorm(x, g, scale, shift):
    y = x * lax.rsqrt(jnp.mean(x * x, axis=-1, keepdims=True) + EPS)
    return (y * g) * (1.0 + scale) + shift


def _rmsnorm(x, g):
    return (x * lax.rsqrt(jnp.mean(x * x, axis=-1, keepdims=True) + EPS)) * g


def _mod_kernel(c_ref, w_ref, b_ref, o_ref):
    c = c_ref[...]
    s = (c * _sigmoid(c)).astype(_bf16)
    o_ref[0] = jnp.dot(s, w_ref[0].astype(_bf16), preferred_element_type=_f32) + b_ref[0]


def _modulation(cvec, w_mod, b_mod):
    tn = 1024
    n_cols = N_MOD * D_MODEL
    return pl.pallas_call(
        _mod_kernel,
        out_shape=jax.ShapeDtypeStruct((DEPTH, MOD_ROWS, n_cols), _f32),
        grid=(DEPTH, n_cols // tn),
        in_specs=[
            pl.BlockSpec((MOD_ROWS, D_MODEL), lambda l, j: (0, 0)),
            pl.BlockSpec((1, D_MODEL, tn), lambda l, j: (l, 0, j)),
            pl.BlockSpec((1, 1, tn), lambda l, j: (l, 0, j)),
        ],
        out_specs=pl.BlockSpec((1, MOD_ROWS, tn), lambda l, j: (l, 0, j)),
        compiler_params=pltpu.CompilerParams(
            dimension_semantics=("arbitrary", "arbitrary"), vmem_limit_bytes=VMEM_LIMIT),
        name="modulation",
    )(cvec, w_mod, b_mod.reshape(DEPTH, 1, n_cols))


def _toeplitz_kernel(rpb_ref, onehot_ref, mask_ref, o_ref):
    o_ref[...] = jnp.dot(rpb_ref[...], onehot_ref[...], preferred_element_type=_f32,
                         precision=lax.Precision.HIGHEST) + mask_ref[...]


def _na_bias_tables(na_rpb):
    n_pair = NA_TOEP_ROWS - 1
    n_col = 2 * NA_KW - 1
    n_col_pad = 32
    cols = np.arange(GRID_W)
    cstart = np.clip(cols - NA_KW // 2, 0, GRID_W - NA_KW)
    in_win = (cols[None, :] >= cstart[:, None]) & (cols[None, :] < cstart[:, None] + NA_KW)
    col_idx = np.clip(cols[None, :] - cols[:, None] + NA_KW - 1, 0, n_col - 1)
    onehot = np.zeros((2, n_col_pad, GRID_W, 2, GRID_W), np.float32)
    for half in range(2):
        onehot[half, :, :, half, :] = (col_idx[None] == np.arange(n_col_pad)[:, None, None]) & in_win[None]
    onehot = onehot.reshape(2 * n_col_pad, 2 * GRID_W * GRID_W)
    mask = np.where(np.broadcast_to(in_win[:, None, :], (GRID_W, 2, GRID_W)), 0.0, -np.inf)
    mask = mask.reshape(1, -1).astype(np.float32)
    rpb_pad = jnp.pad(na_rpb, ((0, 0), (0, 0), (0, 0), (0, n_col_pad - n_col)))
    rpb_pairs = jnp.concatenate([rpb_pad[:, :, :-1], rpb_pad[:, :, 1:]], axis=-1)
    n_rows = DEPTH * NA_HEADS * n_pair
    row_block = 128
    pairs = pl.pallas_call(
        _toeplitz_kernel,
        out_shape=jax.ShapeDtypeStruct((n_rows, onehot.shape[1]), _f32),
        grid=(n_rows // row_block,),
        in_specs=[
            pl.BlockSpec((row_block, 2 * n_col_pad), lambda r: (r, 0)),
            pl.BlockSpec(onehot.shape, lambda r: (0, 0)),
            pl.BlockSpec(mask.shape, lambda r: (0, 0)),
        ],
        out_specs=pl.BlockSpec((row_block, onehot.shape[1]), lambda r: (r, 0)),
        compiler_params=pltpu.CompilerParams(dimension_semantics=("arbitrary",), vmem_limit_bytes=VMEM_LIMIT),
        name="rpb_toeplitz",
    )(rpb_pairs.reshape(n_rows, 2 * n_col_pad), jnp.asarray(onehot), jnp.asarray(mask))
    return pairs.reshape(DEPTH, N_HEAD_BLOCKS, HEADS_PER_BLOCK, n_pair, GRID_W, 2 * GRID_W)


def _rope_tables():
    t = jnp.arange(DEC_SEQ)
    half = NA_HEAD_DIM // 4
    freqs = ROPE_THETA ** (-jnp.arange(half, dtype=_f32) / half)
    ang_r = (t // GRID_W).astype(_f32)[:, None] * freqs[None, :]
    ang_c = (t % GRID_W).astype(_f32)[:, None] * freqs[None, :]
    cos_h = jnp.concatenate([jnp.cos(ang_r)] * 2 + [jnp.cos(ang_c)] * 2, axis=-1)
    sin_h = jnp.concatenate([-jnp.sin(ang_r), jnp.sin(ang_r), -jnp.sin(ang_c), jnp.sin(ang_c)], axis=-1)
    return jnp.tile(cos_h, (1, HEADS_PER_BLOCK)), jnp.tile(sin_h, (1, HEADS_PER_BLOCK))


IN_TM = 1024
IN_TN = 1024
IN_PROJ_COLS = 2 * A_WIDTH + 3 * NA_WIDTH + 2 * D_MODEL
Q_COL_TILE = (2 * A_WIDTH) // IN_TN
K_COL_TILE = (2 * A_WIDTH + NA_WIDTH) // IN_TN
ROW_CHUNK = 256


def _in_kernel(x_ref, shift_ref, scale_ref, g_ref, w_ref, cos_ref, sin_ref, o_ref, h_ref):
    i = pl.program_id(0)
    j = pl.program_id(1)

    @pl.when(j == 0)
    def _():
        def body(c, carry):
            rows = pl.ds(pl.multiple_of(c * ROW_CHUNK, ROW_CHUNK), ROW_CHUNK)
            h = _modnorm(x_ref[rows, :], g_ref[0], scale_ref[0, 0], shift_ref[0, 0])
            h_ref[rows, :] = h.astype(_bf16)
            return carry
        lax.fori_loop(0, IN_TM // ROW_CHUNK, body, 0)

    o_ref[...] = jnp.dot(h_ref[...], w_ref[0].astype(_bf16), preferred_element_type=_f32)
    is_rope = jnp.logical_and(i >= N_CTX_TOK // IN_TM, jnp.logical_or(j == Q_COL_TILE, j == K_COL_TILE))

    @pl.when(is_rope)
    def _():
        lane = lax.broadcasted_iota(jnp.int32, (IN_TM, LANES), 1)
        first_half = (lane % (NA_HEAD_DIM // 2)) < (NA_HEAD_DIM // 4)
        cos = cos_ref[...]
        sin = sin_ref[...]
        for cb in range(IN_TN // LANES):
            cols = slice(cb * LANES, (cb + 1) * LANES)
            xb = o_ref[:, cols]
            partner = jnp.where(first_half, pltpu.roll(xb, LANES - NA_HEAD_DIM // 4, axis=1),
                                pltpu.roll(xb, NA_HEAD_DIM // 4, axis=1))
            o_ref[:, cols] = xb * cos + partner * sin


def _mod_spec(layer, tm, chunk):
    return pl.BlockSpec((1, 1, 1, D_MODEL), lambda i, j: (layer, _mod_row(i, tm), 0, chunk))


def _layer_vec_spec(layer, width):
    return pl.BlockSpec((1, 1, width), lambda *_: (layer, 0, 0))


def _in_proj(x, mod, g, w_in, w_layer, cos_t, sin_t, layer):
    n_m = N_TOK // IN_TM
    return pl.pallas_call(
        _in_kernel,
        out_shape=jax.ShapeDtypeStruct((N_TOK, IN_PROJ_COLS), _f32),
        grid=(n_m, IN_PROJ_COLS // IN_TN),
        in_specs=[
            pl.BlockSpec((IN_TM, D_MODEL), lambda i, j: (i, 0)),
            _mod_spec(layer, IN_TM, 0), _mod_spec(layer, IN_TM, 1),
            _layer_vec_spec(layer, D_MODEL),
            pl.BlockSpec((1, D_MODEL, IN_TN), lambda i, j: (w_layer, 0, j)),
            pl.BlockSpec((DEC_SEQ, LANES), lambda i, j: (0, 0)),
            pl.BlockSpec((DEC_SEQ, LANES), lambda i, j: (0, 0)),
        ],
        out_specs=pl.BlockSpec((IN_TM, IN_TN), lambda i, j: (i, j)),
        scratch_shapes=[pltpu.VMEM((IN_TM, D_MODEL), _bf16)],
        compiler_params=pltpu.CompilerParams(
            dimension_semantics=("arbitrary", "arbitrary"), vmem_limit_bytes=VMEM_LIMIT),
        name="in_proj",
    )(x, mod, mod, g, w_in, cos_t, sin_t)


SGU_TM = 1024


def _sgu_kernel(au_ref, av_ref, lng_ref, lnb_ref, ws_ref, bs_ref, o_ref):
    gdim = A_WIDTH // A_GROUPS

    def body(c, carry):
        rows = pl.ds(pl.multiple_of(c * CHUNK, CHUNK), CHUNK)
        u = _gelu(au_ref[rows, :])
        gv = _gelu(av_ref[rows, :])
        mu = jnp.mean(gv, axis=-1, keepdims=True)
        var = jnp.mean(jnp.square(gv - mu), axis=-1, keepdims=True)
        vn = ((gv - mu) * lax.rsqrt(var + EPS) * lng_ref[0] + lnb_ref[0]).astype(_bf16)
        for g in range(A_GROUPS):
            cols = slice(g * gdim, (g + 1) * gdim)
            s = jnp.dot(ws_ref[0, g], vn[:, cols], preferred_element_type=_f32) + bs_ref[0, g]
            o_ref[rows, cols] = (u[:, cols] * s).astype(_bf16)
        return carry

    lax.fori_loop(0, SGU_TM // CHUNK, body, 0)


def _sgu(proj, ln_g, ln_b, w_s, b_s, layer):
    gdim = A_WIDTH // A_GROUPS
    return pl.pallas_call(
        _sgu_kernel,
        out_shape=jax.ShapeDtypeStruct((N_TOK, A_WIDTH), _bf16),
        grid=(N_TOK // SGU_TM,),
        in_specs=[
            pl.BlockSpec((SGU_TM, A_WIDTH), lambda i: (i, 0)),
            pl.BlockSpec((SGU_TM, A_WIDTH), lambda i: (i, 1)),
            _layer_vec_spec(layer, A_WIDTH),
            _layer_vec_spec(layer, A_WIDTH),
            pl.BlockSpec((1, A_GROUPS, CHUNK, CHUNK), lambda i: (layer, 0, 0, 0)),
            pl.BlockSpec((1, A_GROUPS, CHUNK, gdim), lambda i: (layer, 0, 0, 0)),
        ],
        out_specs=pl.BlockSpec((SGU_TM, A_WIDTH), lambda i: (i, 0)),
        compiler_params=pltpu.CompilerParams(dimension_semantics=("arbitrary",), vmem_limit_bytes=VMEM_LIMIT),
        name="sgu",
    )(proj, proj, ln_g, ln_b, w_s, b_s)


ATT_SCALE = NA_HEAD_DIM ** -0.5
Q_LANE_BLOCK = (2 * A_WIDTH) // LANES
K_LANE_BLOCK = (2 * A_WIDTH + NA_WIDTH) // LANES
V_LANE_BLOCK = (2 * A_WIDTH + 2 * NA_WIDTH) // LANES


def _softmax_pv(parts):
    m = functools.reduce(jnp.maximum, [jnp.max(s, axis=-1, keepdims=True) for s, _ in parts])
    es = [jnp.exp(s - m) for s, _ in parts]
    l = functools.reduce(jnp.add, [jnp.sum(e, axis=-1, keepdims=True) for e in es])
    o = functools.reduce(jnp.add, [jnp.dot(e.astype(_bf16), v, preferred_element_type=_f32)
                                   for e, (_, v) in zip(es, parts)])
    return o / l


CTX_SEQS_PER_STEP = 4
CTX_TM = CTX_SEQS_PER_STEP * SEQ


def _head_lane_masks(rows):
    lane = lax.broadcasted_iota(jnp.int32, (rows, LANES), 1)
    return [jnp.logical_and(lane >= hh * NA_HEAD_DIM, lane < (hh + 1) * NA_HEAD_DIM)
            for hh in range(HEADS_PER_BLOCK)]


def _select_heads(masks, per_head):
    out = per_head[-1]
    for mask, val in zip(masks[:-1], per_head[:-1]):
        out = jnp.where(mask, val, out)
    return out


class _CastJob:
    def __init__(self, w, layer, n_steps, split_rows):
        _, r, c = w.shape
        self.w, self.layer = w, layer
        self.out_shape = jax.ShapeDtypeStruct((r, c), _bf16)
        self.block = (r // n_steps, c) if split_rows else (r, c // n_steps)
        self.split_rows = split_rows

    def specs(self, step_of):
        pos = (lambda *g: (step_of(*g), 0)) if self.split_rows else (lambda *g: (0, step_of(*g)))
        layer = self.layer
        return (pl.BlockSpec((1,) + self.block, lambda *g: (layer,) + pos(*g)), pl.BlockSpec(self.block, pos))


def _run_cast_jobs(src_refs, dst_refs):
    for src, dst in zip(src_refs, dst_refs):
        dst[...] = src[0].astype(_bf16)


def _ctx_attn_kernel(n_cast, q_ref, k_ref, v_ref, *rest):
    outs = rest[len(rest) - 3 - n_cast:]
    o_ref, kst_ref, vst_ref = outs[:3]
    _run_cast_jobs(rest[:n_cast], outs[3:])
    masks = _head_lane_masks(SEQ)
    seq_outs = []
    for s in range(CTX_SEQS_PER_STEP):
        rows = slice(s * SEQ, (s + 1) * SEQ)
        q32 = q_ref[rows, :] * ATT_SCALE
        k32 = k_ref[rows, :]
        v32 = v_ref[rows, :]
        for hh in range(HEADS_PER_BLOCK):
            cols = slice(hh * NA_HEAD_DIM, (hh + 1) * NA_HEAD_DIM)
            kst_ref[s, 0, hh] = k32[:, cols]
            vst_ref[s, 0, hh] = v32[:, cols]
        kb = k32.astype(_bf16)
        vb = v32.astype(_bf16)
        per_head = []
        for mask in masks:
            qh = jnp.where(mask, q32, 0.0).astype(_bf16)
            sc = lax.dot_general(qh, kb, _NT, preferred_element_type=_f32)
            per_head.append(_softmax_pv([(sc, vb)]))
        seq_outs.append(_select_heads(masks, per_head))
    o_ref[...] = jnp.concatenate(seq_outs, axis=0).astype(_bf16)


CTX_GRID = (N_CTX_TOK // CTX_TM, N_HEAD_BLOCKS)
CTX_STEPS = CTX_GRID[0] * CTX_GRID[1]


def _ctx_attention(proj, layer, states, cast_jobs):
    state_shape = jax.ShapeDtypeStruct((BATCH, DEPTH, NA_HEADS, SEQ, NA_HEAD_DIM), _f32)
    state_spec = pl.BlockSpec((CTX_SEQS_PER_STEP, 1, HEADS_PER_BLOCK, SEQ, NA_HEAD_DIM),
                              lambda i, h: (i, layer, h, 0, 0))
    cast_specs = [job.specs(lambda i, h: i * CTX_GRID[1] + h) for job in cast_jobs]
    n_cast, n_state_in = len(cast_jobs), len(states)
    return pl.pallas_call(
        functools.partial(_ctx_attn_kernel, n_cast),
        out_shape=(jax.ShapeDtypeStruct((N_TOK, NA_WIDTH), _bf16), state_shape, state_shape)
        + tuple(job.out_shape for job in cast_jobs),
        grid=CTX_GRID,
        in_specs=[
            pl.BlockSpec((CTX_TM, LANES), lambda i, h: (i, Q_LANE_BLOCK + h)),
            pl.BlockSpec((CTX_TM, LANES), lambda i, h: (i, K_LANE_BLOCK + h)),
            pl.BlockSpec((CTX_TM, LANES), lambda i, h: (i, V_LANE_BLOCK + h)),
        ] + [s[0] for s in cast_specs] + [pl.BlockSpec(memory_space=pl.ANY)] * n_state_in,
        out_specs=(pl.BlockSpec((CTX_TM, LANES), lambda i, h: (i, h)), state_spec, state_spec)
        + tuple(s[1] for s in cast_specs),
        input_output_aliases={3 + n_cast + n: 1 + n for n in range(n_state_in)},
        compiler_params=pltpu.CompilerParams(
            dimension_semantics=("arbitrary", "arbitrary"), vmem_limit_bytes=VMEM_LIMIT),
        name="ctx_attention",
    )(proj, proj, proj, *[job.w for job in cast_jobs], *states)


def _na_pair_bias(bias_ref, hh, blocks, lane_lo):
    pieces = []
    for a, keep_lo, keep_hi in blocks:
        if not (keep_lo or keep_hi):
            pieces.append(jnp.full((GRID_W, 2 * GRID_W), -jnp.inf, _f32))
            continue
        piece = bias_ref[0, 0, hh, a]
        if not keep_lo:
            piece = jnp.where(lane_lo, -jnp.inf, piece)
        if not keep_hi:
            piece = jnp.where(lane_lo, piece, -jnp.inf)
        pieces.append(piece)
    return jnp.concatenate(pieces, axis=-1)


def _na_attn_kernel(n_cast, q_ref, k_ref, v_ref, ck_ref, cv_ref, bias_ref, yb_hbm_ref, *rest):
    del yb_hbm_ref
    o_ref = rest[n_cast]
    _run_cast_jobs(rest[:n_cast], rest[n_cast + 1:])
    lane_lo = lax.broadcasted_iota(jnp.int32, (GRID_W, 2 * GRID_W), 1) < GRID_W
    masks = _head_lane_masks(DEC_SEQ)
    q32 = q_ref[...] * ATT_SCALE
    k = k_ref[...].astype(_bf16)
    v = v_ref[...].astype(_bf16)
    ck = jnp.concatenate([ck_ref[0, 0, hh] for hh in range(HEADS_PER_BLOCK)], axis=-1).astype(_bf16)
    cv = jnp.concatenate([cv_ref[0, 0, hh] for hh in range(HEADS_PER_BLOCK)], axis=-1).astype(_bf16)
    head_outs = []
    for hh, mask in enumerate(masks):
        pair_outs = []
        q = jnp.where(mask, q32, 0.0).astype(_bf16)
        s_ctx = lax.dot_general(q, ck, _NT, preferred_element_type=_f32)
        for p, (w0, n_rows, per_row) in enumerate(NA_PLAN):
            qrows = slice(p * NA_QBLK, (p + 1) * NA_QBLK)
            krows = slice(w0 * GRID_W, (w0 + n_rows) * GRID_W)
            bias = jnp.concatenate([_na_pair_bias(bias_ref, hh, blocks, lane_lo) for blocks in per_row], axis=0)
            s_win = lax.dot_general(q[qrows], k[krows], _NT, preferred_element_type=_f32) + bias
            pair_outs.append(_softmax_pv([(s_win, v[krows]), (s_ctx[qrows], cv)]))
        head_outs.append(jnp.concatenate(pair_outs, axis=0))
    o_ref[...] = _select_heads(masks, head_outs).astype(_bf16)


NA_GRID = (N_HEAD_BLOCKS, DEC_BATCH)
NA_STEPS = NA_GRID[0] * NA_GRID[1]


def _na_attention(proj, cache_k, cache_v, bias_tab, layer, yb, cast_jobs):
    tile0 = N_CTX_TOK // DEC_SEQ
    qkv_spec = lambda lane_block: pl.BlockSpec((DEC_SEQ, LANES), lambda h, b: (tile0 + b, lane_block + h))
    cache_spec = pl.BlockSpec((1, 1, HEADS_PER_BLOCK, PAST_LEN, NA_HEAD_DIM), lambda h, b: (b, layer, h, 0, 0))
    cast_specs = [job.specs(lambda h, b: h * NA_GRID[1] + b) for job in cast_jobs]
    return pl.pallas_call(
        functools.partial(_na_attn_kernel, len(cast_jobs)),
        out_shape=(jax.ShapeDtypeStruct((N_TOK, NA_WIDTH), _bf16),) + tuple(job.out_shape for job in cast_jobs),
        grid=NA_GRID,
        in_specs=[
            qkv_spec(Q_LANE_BLOCK), qkv_spec(K_LANE_BLOCK), qkv_spec(V_LANE_BLOCK),
            cache_spec, cache_spec,
            pl.BlockSpec((1, 1, HEADS_PER_BLOCK, NA_TOEP_ROWS - 1, GRID_W, 2 * GRID_W),
                         lambda h, b: (layer, h, 0, 0, 0, 0)),
            pl.BlockSpec(memory_space=pl.ANY),
        ] + [s[0] for s in cast_specs],
        out_specs=(pl.BlockSpec((DEC_SEQ, LANES), lambda h, b: (tile0 + b, h)),) + tuple(s[1] for s in cast_specs),
        input_output_aliases={6: 0},
        compiler_params=pltpu.CompilerParams(
            dimension_semantics=("arbitrary", "arbitrary"), vmem_limit_bytes=VMEM_LIMIT),
        name="na_attention",
    )(proj, proj, proj, cache_k, cache_v, bias_tab, yb, *[job.w for job in cast_jobs])


MIX_TM = 512
MIX_TC = 512
GA_COL_TILE = (2 * A_WIDTH + 3 * NA_WIDTH) // MIX_TC
GB_COL_TILE = GA_COL_TILE + D_MODEL // MIX_TC


def _mix_kernel(ga_ref, gb_ref, ya_ref, yb_ref, wpa_ref, wpb_ref, wout_ref, x_ref, gate_ref, g_ref, o_ref):
    j = pl.program_id(1)

    @pl.when(j == 0)
    def _():
        o_ref[...] = jnp.zeros_like(o_ref)

    pa = jnp.dot(ya_ref[...], wpa_ref[...], preferred_element_type=_f32)
    pb = jnp.dot(yb_ref[...], wpb_ref[...], preferred_element_type=_f32)
    merged = (_sigmoid(ga_ref[...]) * pa + _sigmoid(gb_ref[...]) * pb).astype(_bf16)
    o_ref[...] += jnp.dot(merged, wout_ref[...], preferred_element_type=_f32)

    @pl.when(j == pl.num_programs(1) - 1)
    def _():
        o_ref[...] = x_ref[...] + gate_ref[0, 0] * _rmsnorm(o_ref[...], g_ref[0])


def _mix(proj, ya, yb, w_pa, w_pb, w_out, x, mod, g_post, layer):
    return pl.pallas_call(
        _mix_kernel,
        out_shape=jax.ShapeDtypeStruct((N_TOK, D_MODEL), _f32),
        grid=(N_TOK // MIX_TM, D_MODEL // MIX_TC),
        in_specs=[
            pl.BlockSpec((MIX_TM, MIX_TC), lambda i, j: (i, GA_COL_TILE + j)),
            pl.BlockSpec((MIX_TM, MIX_TC), lambda i, j: (i, GB_COL_TILE + j)),
            pl.BlockSpec((MIX_TM, A_WIDTH), lambda i, j: (i, 0)),
            pl.BlockSpec((MIX_TM, NA_WIDTH), lambda i, j: (i, 0)),
            pl.BlockSpec((A_WIDTH, MIX_TC), lambda i, j: (0, j)),
            pl.BlockSpec((NA_WIDTH, MIX_TC), lambda i, j: (0, j)),
            pl.BlockSpec((MIX_TC, D_MODEL), lambda i, j: (j, 0)),
            pl.BlockSpec((MIX_TM, D_MODEL), lambda i, j: (i, 0)),
            _mod_spec(layer, MIX_TM, 2),
            _layer_vec_spec(layer, D_MODEL),
        ],
        out_specs=pl.BlockSpec((MIX_TM, D_MODEL), lambda i, j: (i, 0)),
        compiler_params=pltpu.CompilerParams(
            dimension_semantics=("arbitrary", "arbitrary"), vmem_limit_bytes=VMEM_LIMIT),
        name="mix_out",
    )(proj, proj, ya, yb, w_pa, w_pb, w_out, x, mod, g_post)


FFN_TM = 1024
FFN_TF = 512


FFN_OUT_SPLIT = 2


def _ffn_kernel(x_ref, shift_ref, scale_ref, gate_ref, gpre_ref, gpost_ref, w1_ref, w2_ref, o_ref, h_ref):
    k = pl.program_id(1)

    @pl.when(k == 0)
    def _():
        def body(c, carry):
            rows = pl.ds(pl.multiple_of(c * ROW_CHUNK, ROW_CHUNK), ROW_CHUNK)
            h = _modnorm(x_ref[rows, :], gpre_ref[0], scale_ref[0, 0], shift_ref[0, 0])
            h_ref[rows, :] = h.astype(_bf16)
            o_ref[rows, :] = jnp.zeros((ROW_CHUNK, D_MODEL), _f32)
            return carry
        lax.fori_loop(0, FFN_TM // ROW_CHUNK, body, 0)

    a = jnp.dot(h_ref[...], w1_ref[...], preferred_element_type=_f32)
    a = jnp.square(jnp.maximum(a, 0.0)).astype(_bf16)
    width = D_MODEL // FFN_OUT_SPLIT
    for s in range(FFN_OUT_SPLIT):
        cols = slice(s * width, (s + 1) * width)
        o_ref[:, cols] += jnp.dot(a, w2_ref[:, cols], preferred_element_type=_f32)

    @pl.when(k == pl.num_programs(1) - 1)
    def _():
        def body(c, carry):
            rows = pl.ds(pl.multiple_of(c * ROW_CHUNK, ROW_CHUNK), ROW_CHUNK)
            o_ref[rows, :] = x_ref[rows, :] + gate_ref[0, 0] * _rmsnorm(o_ref[rows, :], gpost_ref[0])
            return carry
        lax.fori_loop(0, FFN_TM // ROW_CHUNK, body, 0)


def _ffn(x, mod, g_pre, g_post, w1, w2, layer):
    return pl.pallas_call(
        _ffn_kernel,
        out_shape=jax.ShapeDtypeStruct((N_TOK, D_MODEL), _f32),
        grid=(N_TOK // FFN_TM, D_FF // FFN_TF),
        in_specs=[
            pl.BlockSpec((FFN_TM, D_MODEL), lambda i, k: (i, 0)),
            _mod_spec(layer, FFN_TM, 3), _mod_spec(layer, FFN_TM, 4), _mod_spec(layer, FFN_TM, 5),
            _layer_vec_spec(layer, D_MODEL),
            _layer_vec_spec(layer, D_MODEL),
            pl.BlockSpec((D_MODEL, FFN_TF), lambda i, k: (0, k)),
            pl.BlockSpec((FFN_TF, D_MODEL), lambda i, k: (k, 0)),
        ],
        out_specs=pl.BlockSpec((FFN_TM, D_MODEL), lambda i, k: (i, 0)),
        scratch_shapes=[pltpu.VMEM((FFN_TM, D_MODEL), _bf16)],
        compiler_params=pltpu.CompilerParams(
            dimension_semantics=("arbitrary", "arbitrary"), vmem_limit_bytes=VMEM_LIMIT),
        name="ffn",
    )(x, mod, mod, mod, g_pre, g_post, w1, w2)


def kernel(x_prompt, x_sample, cache_ctx_k, cache_ctx_v, c, c_ctx, w_mod, b_mod, g_pre_mix, g_post_mix, g_pre_ffn,
           g_post_ffn, w_in, sgu_ln_g, sgu_ln_b, sgu_w, sgu_b, na_rpb, w_pa, w_pb, w_out, w_ff1, w_ff2):
    x = jnp.concatenate([x_prompt.reshape(N_CTX_TOK, D_MODEL), x_sample.reshape(DEC_BATCH * DEC_SEQ, D_MODEL)], axis=0)
    cvec = jnp.concatenate([c_ctx[None, :], c, jnp.zeros((MOD_ROWS - 1 - DEC_BATCH, D_MODEL), _f32)], axis=0)
    mod = _modulation(cvec, w_mod, b_mod).reshape(DEPTH, MOD_ROWS, 1, N_MOD * D_MODEL)
    bias = _na_bias_tables(na_rpb)
    cos_t, sin_t = _rope_tables()
    gdim = A_WIDTH // A_GROUPS
    sgu_b_lanes = jnp.broadcast_to(sgu_b[:, :, :, None], (DEPTH, A_GROUPS, CHUNK, gdim))

    vec = lambda a: a[:, None, :]
    g_pre_mix, g_post_mix, g_pre_ffn, g_post_ffn = vec(g_pre_mix), vec(g_post_mix), vec(g_pre_ffn), vec(g_post_ffn)
    sgu_ln_g, sgu_ln_b = vec(sgu_ln_g), vec(sgu_ln_b)
    sgu_w = sgu_w.astype(_bf16)

    states = ()
    w_in_l, w_in_layer = w_in, 0
    for l in range(DEPTH):
        proj = _in_proj(x, mod, g_pre_mix, w_in_l, w_in_layer, cos_t, sin_t, l)
        ya = _sgu(proj, sgu_ln_g, sgu_ln_b, sgu_w, sgu_b_lanes, l)
        ctx_jobs = [_CastJob(w_ff1, l, CTX_STEPS, split_rows=False), _CastJob(w_ff2, l, CTX_STEPS, split_rows=True)]
        if l + 1 < DEPTH:
            ctx_jobs.append(_CastJob(w_in, l + 1, CTX_STEPS, split_rows=True))
        yb, k_state, v_state, w_ff1_l, w_ff2_l, *w_in_next = _ctx_attention(proj, l, states, ctx_jobs)
        states = (k_state, v_state)
        if w_in_next:
            w_in_l, w_in_layer = w_in_next[0][None], 0
        na_jobs = [_CastJob(w, l, NA_STEPS, split_rows=True) for w in (w_pa, w_pb, w_out)]
        yb, w_pa_l, w_pb_l, w_out_l = _na_attention(proj, cache_ctx_k, cache_ctx_v, bias, l, yb, na_jobs)
        x = _mix(proj, ya, yb, w_pa_l, w_pb_l, w_out_l, x, mod, g_post_mix, l)
        x = _ffn(x, mod, g_pre_ffn, g_post_ffn, w_ff1_l, w_ff2_l, l)
    y_prompt = x[:N_CTX_TOK].reshape(BATCH, SEQ, D_MODEL)
    y_sample = x[N_CTX_TOK:].reshape(DEC_BATCH, DEC_SEQ, D_MODEL)
    return (y_prompt, y_sample, states[0], states[1])
```

```python
import functools

import numpy as np
import jax
import jax.numpy as jnp
from jax import lax
from jax.experimental import pallas as pl
from jax.experimental.pallas import tpu as pltpu

D_MODEL = 2048
BATCH = 32
SEQ = 256
DEPTH = 4
DEC_BATCH = 2
DEC_SEQ = 1024
PAST_LEN = 512
GRID_W = 64
CHUNK = 128
A_WIDTH = 1024
A_GROUPS = 8
NA_HEADS = 16
NA_HEAD_DIM = 64
NA_WIDTH = NA_HEADS * NA_HEAD_DIM
NA_KH_MAX = 8
NA_KW = 16
D_FF = 4 * D_MODEL
ROPE_THETA = 10000.0
EPS = 1e-6
N_MOD = 6
IN_COLS = 2 * A_WIDTH + 3 * NA_WIDTH + 2 * D_MODEL

N_CTX_TOK = BATCH * SEQ
N_TOK = N_CTX_TOK + DEC_BATCH * DEC_SEQ
ROWS = DEC_SEQ // GRID_W
NA_KH = min(NA_KH_MAX, ROWS)
MOD_ROWS = 8
LANES = 128
HEADS_PER_BLOCK = LANES // NA_HEAD_DIM
N_HEAD_BLOCKS = NA_HEADS // HEADS_PER_BLOCK

NA_QROWS = 2
NA_QBLK = NA_QROWS * GRID_W
NA_NPAIR = ROWS // NA_QROWS
NA_TOEP_ROWS = 2 * NA_KH_MAX - 1

VMEM_LIMIT = 56 * 1024 * 1024

_f32 = jnp.float32
_bf16 = jnp.bfloat16
_NT = (((1,), (1,)), ((), ()))


def _na_window_plan():
    plan = []
    for p in range(NA_NPAIR):
        rows = [NA_QROWS * p + rr for rr in range(NA_QROWS)]
        starts = [int(np.clip(r - NA_KH // 2, 0, ROWS - NA_KH)) for r in rows]
        w0 = min(starts) // 2 * 2
        w1 = -(-(max(starts) + NA_KH) // 2) * 2
        per_row = []
        for r, rs in zip(rows, starts):
            blocks = []
            for i in range(w0, w1, 2):
                keep_lo = rs <= i < rs + NA_KH
                keep_hi = rs <= i + 1 < rs + NA_KH
                a = i - r + NA_KH_MAX - 1
                assert not (keep_lo or keep_hi) or 0 <= a <= NA_TOEP_ROWS - 2
                blocks.append((a, keep_lo, keep_hi))
            per_row.append(blocks)
        plan.append((w0, w1 - w0, per_row))
    return plan


NA_PLAN = _na_window_plan()


def _mod_row(tile, tm):
    return jnp.maximum((tile * tm - N_CTX_TOK) // DEC_SEQ + 1, 0)


def _sigmoid(x):
    return 1.0 / (1.0 + jnp.exp(-x))


def _gelu(x):
    return 0.5 * x * (1.0 + jnp.tanh(np.sqrt(2.0 / np.pi).astype(np.float32) * (x + 0.044715 * (x * x * x))))


def _modnorm(x, g, scale, shift):
    y = x * lax.rsqrt(jnp.mean(x * x, axis=-1, keepdims=True) + EPS)
    return (y * g) * (1.0 + scale) + shift


def _rmsnorm(x, g):
    return (x * lax.rsqrt(jnp.mean(x * x, axis=-1, keepdims=True) + EPS)) * g


def _mod_kernel(c_ref, w_ref, b_ref, o_ref):
    c = c_ref[...]
    s = (c * _sigmoid(c)).astype(_bf16)
    o_ref[0] = jnp.dot(s, w_ref[0].astype(_bf16), preferred_element_type=_f32) + b_ref[0]


def _modulation(cvec, w_mod, b_mod):
    tn = 1024
    n_cols = N_MOD * D_MODEL
    return pl.pallas_call(
        _mod_kernel,
        out_shape=jax.ShapeDtypeStruct((DEPTH, MOD_ROWS, n_cols), _f32),
        grid=(DEPTH, n_cols // tn),
        in_specs=[
            pl.BlockSpec((MOD_ROWS, D_MODEL), lambda l, j: (0, 0)),
            pl.BlockSpec((1, D_MODEL, tn), lambda l, j: (l, 0, j)),
            pl.BlockSpec((1, 1, tn), lambda l, j: (l, 0, j)),
        ],
        out_specs=pl.BlockSpec((1, MOD_ROWS, tn), lambda l, j: (l, 0, j)),
        compiler_params=pltpu.CompilerParams(
            dimension_semantics=("arbitrary", "arbitrary"), vmem_limit_bytes=VMEM_LIMIT),
        name="modulation",
    )(cvec, w_mod, b_mod.reshape(DEPTH, 1, n_cols))


def _toeplitz_kernel(rpb_ref, onehot_ref, mask_ref, o_ref):
    o_ref[...] = jnp.dot(rpb_ref[...], onehot_ref[...], preferred_element_type=_f32,
                         precision=lax.Precision.HIGHEST) + mask_ref[...]


def _na_bias_tables(na_rpb):
    n_pair = NA_TOEP_ROWS - 1
    n_col = 2 * NA_KW - 1
    n_col_pad = 32
    cols = np.arange(GRID_W)
    cstart = np.clip(cols - NA_KW // 2, 0, GRID_W - NA_KW)
    in_win = (cols[None, :] >= cstart[:, None]) & (cols[None, :] < cstart[:, None] + NA_KW)
    col_idx = np.clip(cols[None, :] - cols[:, None] + NA_KW - 1, 0, n_col - 1)
    onehot = np.zeros((2, n_col_pad, GRID_W, 2, GRID_W), np.float32)
    for half in range(2):
        onehot[half, :, :, half, :] = (col_idx[None] == np.arange(n_col_pad)[:, None, None]) & in_win[None]
    onehot = onehot.reshape(2 * n_col_pad, 2 * GRID_W * GRID_W)
    mask = np.where(np.broadcast_to(in_win[:, None, :], (GRID_W, 2, GRID_W)), 0.0, -np.inf)
    mask = mask.reshape(1, -1).astype(np.float32)
    rpb_pad = jnp.pad(na_rpb, ((0, 0), (0, 0), (0, 0), (0, n_col_pad - n_col)))
    rpb_pairs = jnp.concatenate([rpb_pad[:, :, :-1], rpb_pad[:, :, 1:]], axis=-1)
    n_rows = DEPTH * NA_HEADS * n_pair
    row_block = 128
    pairs = pl.pallas_call(
        _toeplitz_kernel,
        out_shape=jax.ShapeDtypeStruct((n_rows, onehot.shape[1]), _f32),
        grid=(n_rows // row_block,),
        in_specs=[
            pl.BlockSpec((row_block, 2 * n_col_pad), lambda r: (r, 0)),
            pl.BlockSpec(onehot.shape, lambda r: (0, 0)),
            pl.BlockSpec(mask.shape, lambda r: (0, 0)),
        ],
        out_specs=pl.BlockSpec((row_block, onehot.shape[1]), lambda r: (r, 0)),
        compiler_params=pltpu.CompilerParams(dimension_semantics=("arbitrary",), vmem_limit_bytes=VMEM_LIMIT),
        name="rpb_toeplitz",
    )(rpb_pairs.reshape(n_rows, 2 * n_col_pad), jnp.asarray(onehot), jnp.asarray(mask))
    return pairs.reshape(DEPTH, N_HEAD_BLOCKS, HEADS_PER_BLOCK, n_pair, GRID_W, 2 * GRID_W)


def _rope_tables():
    t = jnp.arange(DEC_SEQ)
    half = NA_HEAD_DIM // 4
    freqs = ROPE_THETA ** (-jnp.arange(half, dtype=_f32) / half)
    ang_r = (t // GRID_W).astype(_f32)[:, None] * freqs[None, :]
    ang_c = (t % GRID_W).astype(_f32)[:, None] * freqs[None, :]
    cos_h = jnp.concatenate([jnp.cos(ang_r)] * 2 + [jnp.cos(ang_c)] * 2, axis=-1)
    sin_h = jnp.concatenate([-jnp.sin(ang_r), jnp.sin(ang_r), -jnp.sin(ang_c), jnp.sin(ang_c)], axis=-1)
    return jnp.tile(cos_h, (1, HEADS_PER_BLOCK)), jnp.tile(sin_h, (1, HEADS_PER_BLOCK))


IN_TM = 1024
IN_TN = 1024
IN_PROJ_COLS = 2 * A_WIDTH + 3 * NA_WIDTH + 2 * D_MODEL
Q_COL_TILE = (2 * A_WIDTH) // IN_TN
K_COL_TILE = (2 * A_WIDTH + NA_WIDTH) // IN_TN
ROW_CHUNK = 256


def _in_kernel(x_ref, shift_ref, scale_ref, g_ref, w_ref, cos_ref, sin_ref, o_ref, h_ref):
    i = pl.program_id(0)
    j = pl.program_id(1)

    @pl.when(j == 0)
    def _():
        def body(c, carry):
            rows = pl.ds(pl.multiple_of(c * ROW_CHUNK, ROW_CHUNK), ROW_CHUNK)
            h = _modnorm(x_ref[rows, :], g_ref[0], scale_ref[0, 0], shift_ref[0, 0])
            h_ref[rows, :] = h.astype(_bf16)
            return carry
        lax.fori_loop(0, IN_TM // ROW_CHUNK, body, 0)

    o_ref[...] = jnp.dot(h_ref[...], w_ref[0].astype(_bf16), preferred_element_type=_f32)
    is_rope = jnp.logical_and(i >= N_CTX_TOK // IN_TM, jnp.logical_or(j == Q_COL_TILE, j == K_COL_TILE))

    @pl.when(is_rope)
    def _():
        lane = lax.broadcasted_iota(jnp.int32, (IN_TM, LANES), 1)
        first_half = (lane % (NA_HEAD_DIM // 2)) < (NA_HEAD_DIM // 4)
        cos = cos_ref[...]
        sin = sin_ref[...]
        for cb in range(IN_TN // LANES):
            cols = slice(cb * LANES, (cb + 1) * LANES)
            xb = o_ref[:, cols]
            partner = jnp.where(first_half, pltpu.roll(xb, LANES - NA_HEAD_DIM // 4, axis=1),
                                pltpu.roll(xb, NA_HEAD_DIM // 4, axis=1))
            o_ref[:, cols] = xb * cos + partner * sin


def _mod_spec(layer, tm, chunk, tile0=0):
    return pl.BlockSpec((1, 1, 1, D_MODEL), lambda i, j: (layer, _mod_row(tile0 + i, tm), 0, chunk))


def _layer_vec_spec(layer, width):
    return pl.BlockSpec((1, 1, width), lambda *_: (layer, 0, 0))


def _in_proj(x, mod, g, w_in, w_layer, cos_t, sin_t, layer):
    n_m = N_TOK // IN_TM
    return pl.pallas_call(
        _in_kernel,
        out_shape=jax.ShapeDtypeStruct((N_TOK, IN_PROJ_COLS), _f32),
        grid=(n_m, IN_PROJ_COLS // IN_TN),
        in_specs=[
            pl.BlockSpec((IN_TM, D_MODEL), lambda i, j: (i, 0)),
            _mod_spec(layer, IN_TM, 0), _mod_spec(layer, IN_TM, 1),
            _layer_vec_spec(layer, D_MODEL),
            pl.BlockSpec((1, D_MODEL, IN_TN), lambda i, j: (w_layer, 0, j)),
            pl.BlockSpec((DEC_SEQ, LANES), lambda i, j: (0, 0)),
            pl.BlockSpec((DEC_SEQ, LANES), lambda i, j: (0, 0)),
        ],
        out_specs=pl.BlockSpec((IN_TM, IN_TN), lambda i, j: (i, j)),
        scratch_shapes=[pltpu.VMEM((IN_TM, D_MODEL), _bf16)],
        compiler_params=pltpu.CompilerParams(
            dimension_semantics=("arbitrary", "arbitrary"), vmem_limit_bytes=VMEM_LIMIT),
        name="in_proj",
    )(x, mod, mod, g, w_in, cos_t, sin_t)


SGU_TM = 1024


def _sgu_kernel(au_ref, av_ref, lng_ref, lnb_ref, ws_ref, bs_ref, o_ref):
    gdim = A_WIDTH // A_GROUPS

    def body(c, carry):
        rows = pl.ds(pl.multiple_of(c * CHUNK, CHUNK), CHUNK)
        u = _gelu(au_ref[rows, :])
        gv = _gelu(av_ref[rows, :])
        mu = jnp.mean(gv, axis=-1, keepdims=True)
        var = jnp.mean(jnp.square(gv - mu), axis=-1, keepdims=True)
        vn = ((gv - mu) * lax.rsqrt(var + EPS) * lng_ref[0] + lnb_ref[0]).astype(_bf16)
        for g in range(A_GROUPS):
            cols = slice(g * gdim, (g + 1) * gdim)
            s = jnp.dot(ws_ref[0, g], vn[:, cols], preferred_element_type=_f32) + bs_ref[0, g]
            o_ref[rows, cols] = (u[:, cols] * s).astype(_bf16)
        return carry

    lax.fori_loop(0, SGU_TM // CHUNK, body, 0)


def _sgu(proj, ln_g, ln_b, w_s, b_s, layer):
    gdim = A_WIDTH // A_GROUPS
    return pl.pallas_call(
        _sgu_kernel,
        out_shape=jax.ShapeDtypeStruct((N_TOK, A_WIDTH), _bf16),
        grid=(N_TOK // SGU_TM,),
        in_specs=[
            pl.BlockSpec((SGU_TM, A_WIDTH), lambda i: (i, 0)),
            pl.BlockSpec((SGU_TM, A_WIDTH), lambda i: (i, 1)),
            _layer_vec_spec(layer, A_WIDTH),
            _layer_vec_spec(layer, A_WIDTH),
            pl.BlockSpec((1, A_GROUPS, CHUNK, CHUNK), lambda i: (layer, 0, 0, 0)),
            pl.BlockSpec((1, A_GROUPS, CHUNK, gdim), lambda i: (layer, 0, 0, 0)),
        ],
        out_specs=pl.BlockSpec((SGU_TM, A_WIDTH), lambda i: (i, 0)),
        compiler_params=pltpu.CompilerParams(dimension_semantics=("arbitrary",), vmem_limit_bytes=VMEM_LIMIT),
        name="sgu",
    )(proj, proj, ln_g, ln_b, w_s, b_s)


ATT_SCALE = NA_HEAD_DIM ** -0.5
Q_LANE_BLOCK = (2 * A_WIDTH) // LANES
K_LANE_BLOCK = (2 * A_WIDTH + NA_WIDTH) // LANES
V_LANE_BLOCK = (2 * A_WIDTH + 2 * NA_WIDTH) // LANES


def _softmax_pv(parts):
    m = functools.reduce(jnp.maximum, [jnp.max(s, axis=-1, keepdims=True) for s, _ in parts])
    es = [jnp.exp(s - m) for s, _ in parts]
    l = functools.reduce(jnp.add, [jnp.sum(e, axis=-1, keepdims=True) for e in es])
    o = functools.reduce(jnp.add, [jnp.dot(e.astype(_bf16), v, preferred_element_type=_f32)
                                   for e, (_, v) in zip(es, parts)])
    return o / l


CTX_SEQS_PER_STEP = 4
CTX_TM = CTX_SEQS_PER_STEP * SEQ


def _head_lane_masks(rows):
    lane = lax.broadcasted_iota(jnp.int32, (rows, LANES), 1)
    return [jnp.logical_and(lane >= hh * NA_HEAD_DIM, lane < (hh + 1) * NA_HEAD_DIM)
            for hh in range(HEADS_PER_BLOCK)]


def _select_heads(masks, per_head):
    out = per_head[-1]
    for mask, val in zip(masks[:-1], per_head[:-1]):
        out = jnp.where(mask, val, out)
    return out


class _CastJob:
    def __init__(self, w, layer, n_slabs, split_rows, steps_per_slab=1):
        _, r, c = w.shape
        self.w, self.layer = w, layer
        self.out_shape = jax.ShapeDtypeStruct((r, c), _bf16)
        self.block = (r // n_slabs, c) if split_rows else (r, c // n_slabs)
        self.split_rows = split_rows
        self.steps_per_slab = steps_per_slab

    def specs(self, step_of):
        slab_of = lambda *g: step_of(*g) // self.steps_per_slab
        pos = (lambda *g: (slab_of(*g), 0)) if self.split_rows else (lambda *g: (0, slab_of(*g)))
        layer = self.layer
        return (pl.BlockSpec((1,) + self.block, lambda *g: (layer,) + pos(*g)), pl.BlockSpec(self.block, pos))


def _run_cast_jobs(src_refs, dst_refs):
    for src, dst in zip(src_refs, dst_refs):
        dst[...] = src[0].astype(_bf16)


def _ctx_attn_kernel(n_cast, q_ref, k_ref, v_ref, *rest):
    outs = rest[len(rest) - 3 - n_cast:]
    o_ref, kst_ref, vst_ref = outs[:3]
    _run_cast_jobs(rest[:n_cast], outs[3:])
    masks = _head_lane_masks(SEQ)
    seq_outs = []
    for s in range(CTX_SEQS_PER_STEP):
        rows = slice(s * SEQ, (s + 1) * SEQ)
        q32 = q_ref[rows, :] * ATT_SCALE
        k32 = k_ref[rows, :]
        v32 = v_ref[rows, :]
        for hh in range(HEADS_PER_BLOCK):
            cols = slice(hh * NA_HEAD_DIM, (hh + 1) * NA_HEAD_DIM)
            kst_ref[s, 0, hh] = k32[:, cols]
            vst_ref[s, 0, hh] = v32[:, cols]
        kb = k32.astype(_bf16)
        vb = v32.astype(_bf16)
        per_head = []
        for mask in masks:
            qh = jnp.where(mask, q32, 0.0).astype(_bf16)
            sc = lax.dot_general(qh, kb, _NT, preferred_element_type=_f32)
            per_head.append(_softmax_pv([(sc, vb)]))
        seq_outs.append(_select_heads(masks, per_head))
    o_ref[...] = jnp.concatenate(seq_outs, axis=0).astype(_bf16)


CTX_GRID = (N_CTX_TOK // CTX_TM, N_HEAD_BLOCKS)
CTX_STEPS = CTX_GRID[0] * CTX_GRID[1]


def _ctx_attention(proj, layer, states, cast_jobs):
    state_shape = jax.ShapeDtypeStruct((BATCH, DEPTH, NA_HEADS, SEQ, NA_HEAD_DIM), _f32)
    state_spec = pl.BlockSpec((CTX_SEQS_PER_STEP, 1, HEADS_PER_BLOCK, SEQ, NA_HEAD_DIM),
                              lambda i, h: (i, layer, h, 0, 0))
    cast_specs = [job.specs(lambda i, h: i * CTX_GRID[1] + h) for job in cast_jobs]
    n_cast, n_state_in = len(cast_jobs), len(states)
    return pl.pallas_call(
        functools.partial(_ctx_attn_kernel, n_cast),
        out_shape=(jax.ShapeDtypeStruct((N_TOK, NA_WIDTH), _bf16), state_shape, state_shape)
        + tuple(job.out_shape for job in cast_jobs),
        grid=CTX_GRID,
        in_specs=[
            pl.BlockSpec((CTX_TM, LANES), lambda i, h: (i, Q_LANE_BLOCK + h)),
            pl.BlockSpec((CTX_TM, LANES), lambda i, h: (i, K_LANE_BLOCK + h)),
            pl.BlockSpec((CTX_TM, LANES), lambda i, h: (i, V_LANE_BLOCK + h)),
        ] + [s[0] for s in cast_specs] + [pl.BlockSpec(memory_space=pl.ANY)] * n_state_in,
        out_specs=(pl.BlockSpec((CTX_TM, LANES), lambda i, h: (i, h)), state_spec, state_spec)
        + tuple(s[1] for s in cast_specs),
        input_output_aliases={3 + n_cast + n: 1 + n for n in range(n_state_in)},
        compiler_params=pltpu.CompilerParams(
            dimension_semantics=("arbitrary", "arbitrary"), vmem_limit_bytes=VMEM_LIMIT),
        name="ctx_attention",
    )(proj, proj, proj, *[job.w for job in cast_jobs], *states)


def _na_pair_bias(bias_ref, hh, blocks, lane_lo):
    pieces = []
    for a, keep_lo, keep_hi in blocks:
        if not (keep_lo or keep_hi):
            pieces.append(jnp.full((GRID_W, 2 * GRID_W), -jnp.inf, _f32))
            continue
        piece = bias_ref[0, 0, hh, a]
        if not keep_lo:
            piece = jnp.where(lane_lo, -jnp.inf, piece)
        if not keep_hi:
            piece = jnp.where(lane_lo, piece, -jnp.inf)
        pieces.append(piece)
    return jnp.concatenate(pieces, axis=-1)


def _na_attn_kernel(n_cast, q_ref, k_ref, v_ref, ck_ref, cv_ref, bias_ref, yb_hbm_ref, *rest):
    del yb_hbm_ref
    o_ref = rest[n_cast]
    _run_cast_jobs(rest[:n_cast], rest[n_cast + 1:])
    lane_lo = lax.broadcasted_iota(jnp.int32, (GRID_W, 2 * GRID_W), 1) < GRID_W
    masks = _head_lane_masks(DEC_SEQ)
    q32 = q_ref[...] * ATT_SCALE
    k = k_ref[...].astype(_bf16)
    v = v_ref[...].astype(_bf16)
    ck = jnp.concatenate([ck_ref[0, 0, hh] for hh in range(HEADS_PER_BLOCK)], axis=-1).astype(_bf16)
    cv = jnp.concatenate([cv_ref[0, 0, hh] for hh in range(HEADS_PER_BLOCK)], axis=-1).astype(_bf16)
    head_outs = []
    for hh, mask in enumerate(masks):
        pair_outs = []
        q = jnp.where(mask, q32, 0.0).astype(_bf16)
        s_ctx = lax.dot_general(q, ck, _NT, preferred_element_type=_f32)
        for p, (w0, n_rows, per_row) in enumerate(NA_PLAN):
            qrows = slice(p * NA_QBLK, (p + 1) * NA_QBLK)
            krows = slice(w0 * GRID_W, (w0 + n_rows) * GRID_W)
            bias = jnp.concatenate([_na_pair_bias(bias_ref, hh, blocks, lane_lo) for blocks in per_row], axis=0)
            s_win = lax.dot_general(q[qrows], k[krows], _NT, preferred_element_type=_f32) + bias
            pair_outs.append(_softmax_pv([(s_win, v[krows]), (s_ctx[qrows], cv)]))
        head_outs.append(jnp.concatenate(pair_outs, axis=0))
    o_ref[...] = _select_heads(masks, head_outs).astype(_bf16)


NA_GRID = (N_HEAD_BLOCKS, DEC_BATCH)
NA_STEPS = NA_GRID[0] * NA_GRID[1]


def _na_attention(proj, cache_k, cache_v, bias_tab, layer, yb, cast_jobs):
    tile0 = N_CTX_TOK // DEC_SEQ
    qkv_spec = lambda lane_block: pl.BlockSpec((DEC_SEQ, LANES), lambda h, b: (tile0 + b, lane_block + h))
    cache_spec = pl.BlockSpec((1, 1, HEADS_PER_BLOCK, PAST_LEN, NA_HEAD_DIM), lambda h, b: (b, layer, h, 0, 0))
    cast_specs = [job.specs(lambda h, b: h * NA_GRID[1] + b) for job in cast_jobs]
    return pl.pallas_call(
        functools.partial(_na_attn_kernel, len(cast_jobs)),
        out_shape=(jax.ShapeDtypeStruct((N_TOK, NA_WIDTH), _bf16),) + tuple(job.out_shape for job in cast_jobs),
        grid=NA_GRID,
        in_specs=[
            qkv_spec(Q_LANE_BLOCK), qkv_spec(K_LANE_BLOCK), qkv_spec(V_LANE_BLOCK),
            cache_spec, cache_spec,
            pl.BlockSpec((1, 1, HEADS_PER_BLOCK, NA_TOEP_ROWS - 1, GRID_W, 2 * GRID_W),
                         lambda h, b: (layer, h, 0, 0, 0, 0)),
            pl.BlockSpec(memory_space=pl.ANY),
        ] + [s[0] for s in cast_specs],
        out_specs=(pl.BlockSpec((DEC_SEQ, LANES), lambda h, b: (tile0 + b, h)),) + tuple(s[1] for s in cast_specs),
        input_output_aliases={6: 0},
        compiler_params=pltpu.CompilerParams(
            dimension_semantics=("arbitrary", "arbitrary"), vmem_limit_bytes=VMEM_LIMIT),
        name="na_attention",
    )(proj, proj, proj, cache_k, cache_v, bias_tab, yb, *[job.w for job in cast_jobs])


MIX_TM = 512
MIX_TC = 512
GA_COL_TILE = (2 * A_WIDTH + 3 * NA_WIDTH) // MIX_TC
GB_COL_TILE = GA_COL_TILE + D_MODEL // MIX_TC


def _mix_kernel(ga_ref, gb_ref, ya_ref, yb_ref, wpa_ref, wpb_ref, wout_ref, x_ref, gate_ref, g_ref, o_ref):
    j = pl.program_id(1)

    @pl.when(j == 0)
    def _():
        o_ref[...] = jnp.zeros_like(o_ref)

    pa = jnp.dot(ya_ref[...], wpa_ref[...], preferred_element_type=_f32)
    pb = jnp.dot(yb_ref[...], wpb_ref[...], preferred_element_type=_f32)
    merged = (_sigmoid(ga_ref[...]) * pa + _sigmoid(gb_ref[...]) * pb).astype(_bf16)
    o_ref[...] += jnp.dot(merged, wout_ref[...], preferred_element_type=_f32)

    @pl.when(j == pl.num_programs(1) - 1)
    def _():
        o_ref[...] = x_ref[...] + gate_ref[0, 0] * _rmsnorm(o_ref[...], g_ref[0])


def _mix(proj, ya, yb, w_pa, w_pb, w_out, x, mod, g_post, layer):
    return pl.pallas_call(
        _mix_kernel,
        out_shape=jax.ShapeDtypeStruct((N_TOK, D_MODEL), _f32),
        grid=(N_TOK // MIX_TM, D_MODEL // MIX_TC),
        in_specs=[
            pl.BlockSpec((MIX_TM, MIX_TC), lambda i, j: (i, GA_COL_TILE + j)),
            pl.BlockSpec((MIX_TM, MIX_TC), lambda i, j: (i, GB_COL_TILE + j)),
            pl.BlockSpec((MIX_TM, A_WIDTH), lambda i, j: (i, 0)),
            pl.BlockSpec((MIX_TM, NA_WIDTH), lambda i, j: (i, 0)),
            pl.BlockSpec((A_WIDTH, MIX_TC), lambda i, j: (0, j)),
            pl.BlockSpec((NA_WIDTH, MIX_TC), lambda i, j: (0, j)),
            pl.BlockSpec((MIX_TC, D_MODEL), lambda i, j: (j, 0)),
            pl.BlockSpec((MIX_TM, D_MODEL), lambda i, j: (i, 0)),
            _mod_spec(layer, MIX_TM, 2),
            _layer_vec_spec(layer, D_MODEL),
        ],
        out_specs=pl.BlockSpec((MIX_TM, D_MODEL), lambda i, j: (i, 0)),
        compiler_params=pltpu.CompilerParams(
            dimension_semantics=("arbitrary", "arbitrary"), vmem_limit_bytes=VMEM_LIMIT),
        name="mix_out",
    )(proj, proj, ya, yb, w_pa, w_pb, w_out, x, mod, g_post)


FFN_TM = 1024
FFN_TF = 512


FFN_OUT_SPLIT = 2


def _ffn_kernel(n_cast, x_ref, shift_ref, scale_ref, gate_ref, gpre_ref, gpost_ref, w1_ref, w2_ref, *rest):
    o_ref, h_ref = rest[n_cast], rest[-1]
    _run_cast_jobs(rest[:n_cast], rest[n_cast + 1:-1])
    k = pl.program_id(1)

    @pl.when(k == 0)
    def _():
        def body(c, carry):
            rows = pl.ds(pl.multiple_of(c * ROW_CHUNK, ROW_CHUNK), ROW_CHUNK)
            h = _modnorm(x_ref[rows, :], gpre_ref[0], scale_ref[0, 0], shift_ref[0, 0])
            h_ref[rows, :] = h.astype(_bf16)
            o_ref[rows, :] = jnp.zeros((ROW_CHUNK, D_MODEL), _f32)
            return carry
        lax.fori_loop(0, FFN_TM // ROW_CHUNK, body, 0)

    a = jnp.dot(h_ref[...], w1_ref[...], preferred_element_type=_f32)
    a = jnp.square(jnp.maximum(a, 0.0)).astype(_bf16)
    width = D_MODEL // FFN_OUT_SPLIT
    for s in range(FFN_OUT_SPLIT):
        cols = slice(s * width, (s + 1) * width)
        o_ref[:, cols] += jnp.dot(a, w2_ref[:, cols], preferred_element_type=_f32)

    @pl.when(k == pl.num_programs(1) - 1)
    def _():
        def body(c, carry):
            rows = pl.ds(pl.multiple_of(c * ROW_CHUNK, ROW_CHUNK), ROW_CHUNK)
            o_ref[rows, :] = x_ref[rows, :] + gate_ref[0, 0] * _rmsnorm(o_ref[rows, :], gpost_ref[0])
            return carry
        lax.fori_loop(0, FFN_TM // ROW_CHUNK, body, 0)


FFN_K_STEPS = D_FF // FFN_TF
FFN_CAST_TILES = 8
FFN_CAST_STEPS = FFN_CAST_TILES * FFN_K_STEPS


def _ffn_cast_step(i, k):
    return jnp.where(i < FFN_CAST_TILES, i * FFN_K_STEPS + k, FFN_CAST_STEPS - 1)


def _ffn(x, mod, g_pre, g_post, w1, w2, layer, tile0=0, n_tiles=N_TOK // FFN_TM, cast_jobs=()):
    assert not cast_jobs or n_tiles >= FFN_CAST_TILES
    cast_specs = [job.specs(_ffn_cast_step) for job in cast_jobs]
    outs = pl.pallas_call(
        functools.partial(_ffn_kernel, len(cast_jobs)),
        out_shape=(jax.ShapeDtypeStruct((n_tiles * FFN_TM, D_MODEL), _f32),)
        + tuple(job.out_shape for job in cast_jobs),
        grid=(n_tiles, FFN_K_STEPS),
        in_specs=[
            pl.BlockSpec((FFN_TM, D_MODEL), lambda i, k: (tile0 + i, 0)),
            _mod_spec(layer, FFN_TM, 3, tile0), _mod_spec(layer, FFN_TM, 4, tile0), _mod_spec(layer, FFN_TM, 5, tile0),
            _layer_vec_spec(layer, D_MODEL),
            _layer_vec_spec(layer, D_MODEL),
            pl.BlockSpec((D_MODEL, FFN_TF), lambda i, k: (0, k)),
            pl.BlockSpec((FFN_TF, D_MODEL), lambda i, k: (k, 0)),
        ] + [s[0] for s in cast_specs],
        out_specs=(pl.BlockSpec((FFN_TM, D_MODEL), lambda i, k: (i, 0)),) + tuple(s[1] for s in cast_specs),
        scratch_shapes=[pltpu.VMEM((FFN_TM, D_MODEL), _bf16)],
        compiler_params=pltpu.CompilerParams(
            dimension_semantics=("arbitrary", "arbitrary"), vmem_limit_bytes=VMEM_LIMIT),
        name="ffn",
    )(x, mod, mod, mod, g_pre, g_post, w1, w2, *[job.w for job in cast_jobs])
    return outs


def kernel(x_prompt, x_sample, cache_ctx_k, cache_ctx_v, c, c_ctx, w_mod, b_mod, g_pre_mix, g_post_mix, g_pre_ffn,
           g_post_ffn, w_in, sgu_ln_g, sgu_ln_b, sgu_w, sgu_b, na_rpb, w_pa, w_pb, w_out, w_ff1, w_ff2):
    x = jnp.concatenate([x_prompt.reshape(N_CTX_TOK, D_MODEL), x_sample.reshape(DEC_BATCH * DEC_SEQ, D_MODEL)], axis=0)
    cvec = jnp.concatenate([c_ctx[None, :], c, jnp.zeros((MOD_ROWS - 1 - DEC_BATCH, D_MODEL), _f32)], axis=0)
    mod = _modulation(cvec, w_mod, b_mod).reshape(DEPTH, MOD_ROWS, 1, N_MOD * D_MODEL)
    bias = _na_bias_tables(na_rpb)
    cos_t, sin_t = _rope_tables()
    gdim = A_WIDTH // A_GROUPS
    sgu_b_lanes = jnp.broadcast_to(sgu_b[:, :, :, None], (DEPTH, A_GROUPS, CHUNK, gdim))

    vec = lambda a: a[:, None, :]
    g_pre_mix, g_post_mix, g_pre_ffn, g_post_ffn = vec(g_pre_mix), vec(g_post_mix), vec(g_pre_ffn), vec(g_post_ffn)
    sgu_ln_g, sgu_ln_b = vec(sgu_ln_g), vec(sgu_ln_b)
    sgu_w = sgu_w.astype(_bf16)

    states = ()
    for l in range(DEPTH):
        if l == 0:
            w_in_l = w_in
            ctx_jobs = [_CastJob(w_ff1, 0, CTX_STEPS, split_rows=False), _CastJob(w_ff2, 0, CTX_STEPS, split_rows=True)]
            na_jobs = [_CastJob(w, 0, NA_STEPS, split_rows=True) for w in (w_pa, w_pb, w_out)]
        else:
            w_in_l, ctx_jobs, na_jobs = w_in_b[None], [], []
        proj = _in_proj(x, mod, g_pre_mix, w_in_l, 0, cos_t, sin_t, l)
        ya = _sgu(proj, sgu_ln_g, sgu_ln_b, sgu_w, sgu_b_lanes, l)
        yb, k_state, v_state, *ctx_cast = _ctx_attention(proj, l, states, ctx_jobs)
        states = (k_state, v_state)
        yb, *na_cast = _na_attention(proj, cache_ctx_k, cache_ctx_v, bias, l, yb, na_jobs)
        if l == 0:
            (w_ff1_b, w_ff2_b), (w_pa_b, w_pb_b, w_out_b) = ctx_cast, na_cast
        x = _mix(proj, ya, yb, w_pa_b, w_pb_b, w_out_b, x, mod, g_post_mix, l)
        if l + 1 < DEPTH:
            nxt = l + 1
            jobs = [_CastJob(w, nxt, FFN_CAST_STEPS, split_rows=True) for w in (w_in, w_ff1, w_ff2, w_out)]
            jobs += [_CastJob(w, nxt, FFN_CAST_STEPS // 2, split_rows=True, steps_per_slab=2) for w in (w_pa, w_pb)]
            x, w_in_b, w_ff1_b, w_ff2_b, w_out_b, w_pa_b, w_pb_b = _ffn(
                x, mod, g_pre_ffn, g_post_ffn, w_ff1_b, w_ff2_b, l, cast_jobs=jobs)
        else:
            n_ctx_tiles = N_CTX_TOK // FFN_TM
            last = functools.partial(_ffn, x, mod, g_pre_ffn, g_post_ffn, w_ff1_b, w_ff2_b, l)
            y_prompt, = last(tile0=0, n_tiles=n_ctx_tiles)
            y_sample, = last(tile0=n_ctx_tiles, n_tiles=N_TOK // FFN_TM - n_ctx_tiles)
    return (y_prompt.reshape(BATCH, SEQ, D_MODEL), y_sample.reshape(DEC_BATCH, DEC_SEQ, D_MODEL), states[0], states[1])
```

```python
import functools

import numpy as np
import jax
import jax.numpy as jnp
from jax import lax
from jax.experimental import pallas as pl
from jax.experimental.pallas import tpu as pltpu

D_MODEL = 2048
BATCH = 32
SEQ = 256
DEPTH = 4
DEC_BATCH = 2
DEC_SEQ = 1024
PAST_LEN = 512
GRID_W = 64
CHUNK = 128
A_WIDTH = 1024
A_GROUPS = 8
NA_HEADS = 16
NA_HEAD_DIM = 64
NA_WIDTH = NA_HEADS * NA_HEAD_DIM
NA_KH_MAX = 8
NA_KW = 16
D_FF = 4 * D_MODEL
ROPE_THETA = 10000.0
EPS = 1e-6
N_MOD = 6
IN_COLS = 2 * A_WIDTH + 3 * NA_WIDTH + 2 * D_MODEL

N_CTX_TOK = BATCH * SEQ
N_TOK = N_CTX_TOK + DEC_BATCH * DEC_SEQ
ROWS = DEC_SEQ // GRID_W
NA_KH = min(NA_KH_MAX, ROWS)
MOD_ROWS = 8
LANES = 128
HEADS_PER_BLOCK = LANES // NA_HEAD_DIM
N_HEAD_BLOCKS = NA_HEADS // HEADS_PER_BLOCK

NA_QROWS = 2
NA_QBLK = NA_QROWS * GRID_W
NA_NPAIR = ROWS // NA_QROWS
NA_TOEP_ROWS = 2 * NA_KH_MAX - 1

VMEM_LIMIT = 56 * 1024 * 1024

_f32 = jnp.float32
_bf16 = jnp.bfloat16
_NT = (((1,), (1,)), ((), ()))


def _na_window_plan():
    plan = []
    for p in range(NA_NPAIR):
        rows = [NA_QROWS * p + rr for rr in range(NA_QROWS)]
        starts = [int(np.clip(r - NA_KH // 2, 0, ROWS - NA_KH)) for r in rows]
        w0 = min(starts) // 2 * 2
        w1 = -(-(max(starts) + NA_KH) // 2) * 2
        per_row = []
        for r, rs in zip(rows, starts):
            blocks = []
            for i in range(w0, w1, 2):
                keep_lo = rs <= i < rs + NA_KH
                keep_hi = rs <= i + 1 < rs + NA_KH
                a = i - r + NA_KH_MAX - 1
                assert not (keep_lo or keep_hi) or 0 <= a <= NA_TOEP_ROWS - 2
                blocks.append((a, keep_lo, keep_hi))
            per_row.append(blocks)
        plan.append((w0, w1 - w0, per_row))
    return plan


NA_PLAN = _na_window_plan()


def _mod_row(tile, tm):
    return jnp.maximum((tile * tm - N_CTX_TOK) // DEC_SEQ + 1, 0)


def _sigmoid(x):
    return 1.0 / (1.0 + jnp.exp(-x))


def _gelu(x):
    return 0.5 * x * (1.0 + jnp.tanh(np.sqrt(2.0 / np.pi).astype(np.float32) * (x + 0.044715 * (x * x * x))))


SUMSQ_ROWS = 256
NORM_ROWS = 64
N_LANE_BLOCKS = D_MODEL // LANES


def _row_sumsq_pass(src_ref, ss_ref, n_rows):
    def body(c, carry):
        rows = pl.ds(pl.multiple_of(c * SUMSQ_ROWS, SUMSQ_ROWS), SUMSQ_ROWS)
        acc = jnp.zeros((SUMSQ_ROWS, LANES), _f32)
        for cb in range(N_LANE_BLOCKS):
            v = src_ref[rows, cb * LANES:(cb + 1) * LANES]
            acc = acc + v * v
        ss_ref[rows, :] = jnp.broadcast_to(jnp.sum(acc, axis=-1, keepdims=True), (SUMSQ_ROWS, LANES))
        return carry
    lax.fori_loop(0, n_rows // SUMSQ_ROWS, body, 0)


def _row_rescale_pass(ss_ref, n_rows, block_fn):
    def body(c, carry):
        rows = pl.ds(pl.multiple_of(c * NORM_ROWS, NORM_ROWS), NORM_ROWS)
        r = lax.rsqrt(ss_ref[rows, :] * (1.0 / D_MODEL) + EPS)
        for cb in range(N_LANE_BLOCKS):
            block_fn(rows, slice(cb * LANES, (cb + 1) * LANES), r)
        return carry
    lax.fori_loop(0, n_rows // NORM_ROWS, body, 0)


def _modnorm_to(x_ref, h_ref, ss_ref, n_rows, g_ref, scale_ref, shift_ref, also=None):
    _row_sumsq_pass(x_ref, ss_ref, n_rows)

    def block(rows, cols, r):
        y = x_ref[rows, cols] * r
        h = (y * g_ref[0, :, cols]) * (1.0 + scale_ref[0, 0, :, cols]) + shift_ref[0, 0, :, cols]
        h_ref[rows, cols] = h.astype(_bf16)
        if also is not None:
            also(rows, cols)
    _row_rescale_pass(ss_ref, n_rows, block)


def _residual_rmsnorm_inplace(o_ref, x_ref, ss_ref, n_rows, gate_ref, g_ref):
    _row_sumsq_pass(o_ref, ss_ref, n_rows)

    def block(rows, cols, r):
        y = (o_ref[rows, cols] * r) * g_ref[0, :, cols]
        o_ref[rows, cols] = x_ref[rows, cols] + gate_ref[0, 0, :, cols] * y
    _row_rescale_pass(ss_ref, n_rows, block)


def _mod_kernel(c_ref, w_ref, b_ref, o_ref):
    c = c_ref[...]
    s = (c * _sigmoid(c)).astype(_bf16)
    o_ref[0] = jnp.dot(s, w_ref[0].astype(_bf16), preferred_element_type=_f32) + b_ref[0]


def _modulation(cvec, w_mod, b_mod):
    tn = 1024
    n_cols = N_MOD * D_MODEL
    return pl.pallas_call(
        _mod_kernel,
        out_shape=jax.ShapeDtypeStruct((DEPTH, MOD_ROWS, n_cols), _f32),
        grid=(DEPTH, n_cols // tn),
        in_specs=[
            pl.BlockSpec((MOD_ROWS, D_MODEL), lambda l, j: (0, 0)),
            pl.BlockSpec((1, D_MODEL, tn), lambda l, j: (l, 0, j)),
            pl.BlockSpec((1, 1, tn), lambda l, j: (l, 0, j)),
        ],
        out_specs=pl.BlockSpec((1, MOD_ROWS, tn), lambda l, j: (l, 0, j)),
        compiler_params=pltpu.CompilerParams(
            dimension_semantics=("arbitrary", "arbitrary"), vmem_limit_bytes=VMEM_LIMIT),
        name="modulation",
    )(cvec, w_mod, b_mod.reshape(DEPTH, 1, n_cols))


def _toeplitz_kernel(rpb_ref, onehot_ref, mask_ref, o_ref):
    o_ref[...] = jnp.dot(rpb_ref[...], onehot_ref[...], preferred_element_type=_f32,
                         precision=lax.Precision.HIGHEST) + mask_ref[...]


def _na_bias_tables(na_rpb):
    n_pair = NA_TOEP_ROWS - 1
    n_col = 2 * NA_KW - 1
    n_col_pad = 32
    cols = np.arange(GRID_W)
    cstart = np.clip(cols - NA_KW // 2, 0, GRID_W - NA_KW)
    in_win = (cols[None, :] >= cstart[:, None]) & (cols[None, :] < cstart[:, None] + NA_KW)
    col_idx = np.clip(cols[None, :] - cols[:, None] + NA_KW - 1, 0, n_col - 1)
    onehot = np.zeros((2, n_col_pad, GRID_W, 2, GRID_W), np.float32)
    for half in range(2):
        onehot[half, :, :, half, :] = (col_idx[None] == np.arange(n_col_pad)[:, None, None]) & in_win[None]
    onehot = onehot.reshape(2 * n_col_pad, 2 * GRID_W * GRID_W)
    mask = np.where(np.broadcast_to(in_win[:, None, :], (GRID_W, 2, GRID_W)), 0.0, -np.inf)
    mask = mask.reshape(1, -1).astype(np.float32)
    rpb_pad = jnp.pad(na_rpb, ((0, 0), (0, 0), (0, 0), (0, n_col_pad - n_col)))
    rpb_pairs = jnp.concatenate([rpb_pad[:, :, :-1], rpb_pad[:, :, 1:]], axis=-1)
    n_rows = DEPTH * NA_HEADS * n_pair
    row_block = 128
    pairs = pl.pallas_call(
        _toeplitz_kernel,
        out_shape=jax.ShapeDtypeStruct((n_rows, onehot.shape[1]), _f32),
        grid=(n_rows // row_block,),
        in_specs=[
            pl.BlockSpec((row_block, 2 * n_col_pad), lambda r: (r, 0)),
            pl.BlockSpec(onehot.shape, lambda r: (0, 0)),
            pl.BlockSpec(mask.shape, lambda r: (0, 0)),
        ],
        out_specs=pl.BlockSpec((row_block, onehot.shape[1]), lambda r: (r, 0)),
        compiler_params=pltpu.CompilerParams(dimension_semantics=("arbitrary",), vmem_limit_bytes=VMEM_LIMIT),
        name="rpb_toeplitz",
    )(rpb_pairs.reshape(n_rows, 2 * n_col_pad), jnp.asarray(onehot), jnp.asarray(mask))
    return pairs.reshape(DEPTH, N_HEAD_BLOCKS, HEADS_PER_BLOCK, n_pair, GRID_W, 2 * GRID_W)


def _rope_tables():
    t = jnp.arange(DEC_SEQ)
    half = NA_HEAD_DIM // 4
    freqs = ROPE_THETA ** (-jnp.arange(half, dtype=_f32) / half)
    ang_r = (t // GRID_W).astype(_f32)[:, None] * freqs[None, :]
    ang_c = (t % GRID_W).astype(_f32)[:, None] * freqs[None, :]
    cos_h = jnp.concatenate([jnp.cos(ang_r)] * 2 + [jnp.cos(ang_c)] * 2, axis=-1)
    sin_h = jnp.concatenate([-jnp.sin(ang_r), jnp.sin(ang_r), -jnp.sin(ang_c), jnp.sin(ang_c)], axis=-1)
    return jnp.tile(cos_h, (1, HEADS_PER_BLOCK)), jnp.tile(sin_h, (1, HEADS_PER_BLOCK))


IN_TM = 1024
IN_TN = 1024
IN_PROJ_COLS = 2 * A_WIDTH + 3 * NA_WIDTH + 2 * D_MODEL
Q_COL_TILE = (2 * A_WIDTH) // IN_TN
K_COL_TILE = (2 * A_WIDTH + NA_WIDTH) // IN_TN


def _in_kernel(x_ref, shift_ref, scale_ref, g_ref, w_ref, cos_ref, sin_ref, o_ref, h_ref, ss_ref):
    i = pl.program_id(0)
    j = pl.program_id(1)

    @pl.when(j == 0)
    def _():
        _modnorm_to(x_ref, h_ref, ss_ref, IN_TM, g_ref, scale_ref, shift_ref)

    o_ref[...] = jnp.dot(h_ref[...], w_ref[0].astype(_bf16), preferred_element_type=_f32)
    is_rope = jnp.logical_and(i >= N_CTX_TOK // IN_TM, jnp.logical_or(j == Q_COL_TILE, j == K_COL_TILE))

    @pl.when(is_rope)
    def _():
        lane = lax.broadcasted_iota(jnp.int32, (IN_TM, LANES), 1)
        first_half = (lane % (NA_HEAD_DIM // 2)) < (NA_HEAD_DIM // 4)
        cos = cos_ref[...]
        sin = sin_ref[...]
        for cb in range(IN_TN // LANES):
            cols = slice(cb * LANES, (cb + 1) * LANES)
            xb = o_ref[:, cols]
            partner = jnp.where(first_half, pltpu.roll(xb, LANES - NA_HEAD_DIM // 4, axis=1),
                                pltpu.roll(xb, NA_HEAD_DIM // 4, axis=1))
            o_ref[:, cols] = xb * cos + partner * sin


def _mod_spec(layer, tm, chunk, tile0=0):
    return pl.BlockSpec((1, 1, 1, D_MODEL), lambda i, j: (layer, _mod_row(tile0 + i, tm), 0, chunk))


def _layer_vec_spec(layer, width):
    return pl.BlockSpec((1, 1, width), lambda *_: (layer, 0, 0))


def _in_proj(x, mod, g, w_in, w_layer, cos_t, sin_t, layer):
    n_m = N_TOK // IN_TM
    return pl.pallas_call(
        _in_kernel,
        out_shape=jax.ShapeDtypeStruct((N_TOK, IN_PROJ_COLS), _f32),
        grid=(n_m, IN_PROJ_COLS // IN_TN),
        in_specs=[
            pl.BlockSpec((IN_TM, D_MODEL), lambda i, j: (i, 0)),
            _mod_spec(layer, IN_TM, 0), _mod_spec(layer, IN_TM, 1),
            _layer_vec_spec(layer, D_MODEL),
            pl.BlockSpec((1, D_MODEL, IN_TN), lambda i, j: (w_layer, 0, j)),
            pl.BlockSpec((DEC_SEQ, LANES), lambda i, j: (0, 0)),
            pl.BlockSpec((DEC_SEQ, LANES), lambda i, j: (0, 0)),
        ],
        out_specs=pl.BlockSpec((IN_TM, IN_TN), lambda i, j: (i, j)),
        scratch_shapes=[pltpu.VMEM((IN_TM, D_MODEL), _bf16), pltpu.VMEM((IN_TM, LANES), _f32)],
        compiler_params=pltpu.CompilerParams(
            dimension_semantics=("arbitrary", "arbitrary"), vmem_limit_bytes=VMEM_LIMIT),
        name="in_proj",
    )(x, mod, mod, g, w_in, cos_t, sin_t)


SGU_TM = 1024


def _sgu_kernel(au_ref, av_ref, lng_ref, lnb_ref, ws_ref, bs_ref, o_ref):
    gdim = A_WIDTH // A_GROUPS

    def body(c, carry):
        rows = pl.ds(pl.multiple_of(c * CHUNK, CHUNK), CHUNK)
        u = _gelu(au_ref[rows, :])
        gv = _gelu(av_ref[rows, :])
        mu = jnp.mean(gv, axis=-1, keepdims=True)
        var = jnp.mean(jnp.square(gv - mu), axis=-1, keepdims=True)
        vn = ((gv - mu) * lax.rsqrt(var + EPS) * lng_ref[0] + lnb_ref[0]).astype(_bf16)
        for g in range(A_GROUPS):
            cols = slice(g * gdim, (g + 1) * gdim)
            s = jnp.dot(ws_ref[0, g], vn[:, cols], preferred_element_type=_f32) + bs_ref[0, g]
            o_ref[rows, cols] = (u[:, cols] * s).astype(_bf16)
        return carry

    lax.fori_loop(0, SGU_TM // CHUNK, body, 0)


def _sgu(proj, ln_g, ln_b, w_s, b_s, layer):
    gdim = A_WIDTH // A_GROUPS
    return pl.pallas_call(
        _sgu_kernel,
        out_shape=jax.ShapeDtypeStruct((N_TOK, A_WIDTH), _bf16),
        grid=(N_TOK // SGU_TM,),
        in_specs=[
            pl.BlockSpec((SGU_TM, A_WIDTH), lambda i: (i, 0)),
            pl.BlockSpec((SGU_TM, A_WIDTH), lambda i: (i, 1)),
            _layer_vec_spec(layer, A_WIDTH),
            _layer_vec_spec(layer, A_WIDTH),
            pl.BlockSpec((1, A_GROUPS, CHUNK, CHUNK), lambda i: (layer, 0, 0, 0)),
            pl.BlockSpec((1, A_GROUPS, CHUNK, gdim), lambda i: (layer, 0, 0, 0)),
        ],
        out_specs=pl.BlockSpec((SGU_TM, A_WIDTH), lambda i: (i, 0)),
        compiler_params=pltpu.CompilerParams(dimension_semantics=("arbitrary",), vmem_limit_bytes=VMEM_LIMIT),
        name="sgu",
    )(proj, proj, ln_g, ln_b, w_s, b_s)


ATT_SCALE = NA_HEAD_DIM ** -0.5
Q_LANE_BLOCK = (2 * A_WIDTH) // LANES
K_LANE_BLOCK = (2 * A_WIDTH + NA_WIDTH) // LANES
V_LANE_BLOCK = (2 * A_WIDTH + 2 * NA_WIDTH) // LANES


def _softmax_pv(parts):
    m = functools.reduce(jnp.maximum, [jnp.max(s, axis=-1, keepdims=True) for s, _ in parts])
    es = [jnp.exp(s - m) for s, _ in parts]
    l = functools.reduce(jnp.add, [jnp.sum(e, axis=-1, keepdims=True) for e in es])
    o = functools.reduce(jnp.add, [jnp.dot(e.astype(_bf16), v, preferred_element_type=_f32)
                                   for e, (_, v) in zip(es, parts)])
    return o / l


CTX_SEQS_PER_STEP = 4
CTX_TM = CTX_SEQS_PER_STEP * SEQ


def _head_lane_masks(rows):
    lane = lax.broadcasted_iota(jnp.int32, (rows, LANES), 1)
    return [jnp.logical_and(lane >= hh * NA_HEAD_DIM, lane < (hh + 1) * NA_HEAD_DIM)
            for hh in range(HEADS_PER_BLOCK)]


def _select_heads(masks, per_head):
    out = per_head[-1]
    for mask, val in zip(masks[:-1], per_head[:-1]):
        out = jnp.where(mask, val, out)
    return out


class _CastJob:
    def __init__(self, w, layer, n_slabs, split_rows, steps_per_slab=1):
        _, r, c = w.shape
        self.w, self.layer = w, layer
        self.out_shape = jax.ShapeDtypeStruct((r, c), _bf16)
        self.block = (r // n_slabs, c) if split_rows else (r, c // n_slabs)
        self.split_rows = split_rows
        self.steps_per_slab = steps_per_slab

    def specs(self, step_of):
        slab_of = lambda *g: step_of(*g) // self.steps_per_slab
        pos = (lambda *g: (slab_of(*g), 0)) if self.split_rows else (lambda *g: (0, slab_of(*g)))
        layer = self.layer
        return (pl.BlockSpec((1,) + self.block, lambda *g: (layer,) + pos(*g)), pl.BlockSpec(self.block, pos))


def _run_cast_jobs(src_refs, dst_refs):
    for src, dst in zip(src_refs, dst_refs):
        dst[...] = src[0].astype(_bf16)


def _ctx_attn_kernel(n_cast, q_ref, k_ref, v_ref, *rest):
    outs = rest[len(rest) - 3 - n_cast:]
    o_ref, kst_ref, vst_ref = outs[:3]
    _run_cast_jobs(rest[:n_cast], outs[3:])
    masks = _head_lane_masks(SEQ)
    seq_outs = []
    for s in range(CTX_SEQS_PER_STEP):
        rows = slice(s * SEQ, (s + 1) * SEQ)
        q32 = q_ref[rows, :] * ATT_SCALE
        k32 = k_ref[rows, :]
        v32 = v_ref[rows, :]
        for hh in range(HEADS_PER_BLOCK):
            cols = slice(hh * NA_HEAD_DIM, (hh + 1) * NA_HEAD_DIM)
            kst_ref[s, 0, hh] = k32[:, cols]
            vst_ref[s, 0, hh] = v32[:, cols]
        kb = k32.astype(_bf16)
        vb = v32.astype(_bf16)
        per_head = []
        for mask in masks:
            qh = jnp.where(mask, q32, 0.0).astype(_bf16)
            sc = lax.dot_general(qh, kb, _NT, preferred_element_type=_f32)
            per_head.append(_softmax_pv([(sc, vb)]))
        seq_outs.append(_select_heads(masks, per_head))
    o_ref[...] = jnp.concatenate(seq_outs, axis=0).astype(_bf16)


CTX_GRID = (N_CTX_TOK // CTX_TM, N_HEAD_BLOCKS)
CTX_STEPS = CTX_GRID[0] * CTX_GRID[1]


def _ctx_attention(proj, layer, states, cast_jobs):
    state_shape = jax.ShapeDtypeStruct((BATCH, DEPTH, NA_HEADS, SEQ, NA_HEAD_DIM), _f32)
    state_spec = pl.BlockSpec((CTX_SEQS_PER_STEP, 1, HEADS_PER_BLOCK, SEQ, NA_HEAD_DIM),
                              lambda i, h: (i, layer, h, 0, 0))
    cast_specs = [job.specs(lambda i, h: i * CTX_GRID[1] + h) for job in cast_jobs]
    n_cast, n_state_in = len(cast_jobs), len(states)
    return pl.pallas_call(
        functools.partial(_ctx_attn_kernel, n_cast),
        out_shape=(jax.ShapeDtypeStruct((N_TOK, NA_WIDTH), _bf16), state_shape, state_shape)
        + tuple(job.out_shape for job in cast_jobs),
        grid=CTX_GRID,
        in_specs=[
            pl.BlockSpec((CTX_TM, LANES), lambda i, h: (i, Q_LANE_BLOCK + h)),
            pl.BlockSpec((CTX_TM, LANES), lambda i, h: (i, K_LANE_BLOCK + h)),
            pl.BlockSpec((CTX_TM, LANES), lambda i, h: (i, V_LANE_BLOCK + h)),
        ] + [s[0] for s in cast_specs] + [pl.BlockSpec(memory_space=pl.ANY)] * n_state_in,
        out_specs=(pl.BlockSpec((CTX_TM, LANES), lambda i, h: (i, h)), state_spec, state_spec)
        + tuple(s[1] for s in cast_specs),
        input_output_aliases={3 + n_cast + n: 1 + n for n in range(n_state_in)},
        compiler_params=pltpu.CompilerParams(
            dimension_semantics=("arbitrary", "arbitrary"), vmem_limit_bytes=VMEM_LIMIT),
        name="ctx_attention",
    )(proj, proj, proj, *[job.w for job in cast_jobs], *states)


def _na_pair_bias(bias_ref, hh, blocks, lane_lo):
    pieces = []
    for a, keep_lo, keep_hi in blocks:
        if not (keep_lo or keep_hi):
            pieces.append(jnp.full((GRID_W, 2 * GRID_W), -jnp.inf, _f32))
            continue
        piece = bias_ref[0, 0, hh, a]
        if not keep_lo:
            piece = jnp.where(lane_lo, -jnp.inf, piece)
        if not keep_hi:
            piece = jnp.where(lane_lo, piece, -jnp.inf)
        pieces.append(piece)
    return jnp.concatenate(pieces, axis=-1)


def _na_attn_kernel(n_cast, q_ref, k_ref, v_ref, ck_ref, cv_ref, bias_ref, yb_hbm_ref, *rest):
    del yb_hbm_ref
    o_ref, e_refs = rest[n_cast], rest[-1]
    _run_cast_jobs(rest[:n_cast], rest[n_cast + 1:-1])
    lane_lo = lax.broadcasted_iota(jnp.int32, (GRID_W, 2 * GRID_W), 1) < GRID_W
    masks = _head_lane_masks(DEC_SEQ)
    q32 = q_ref[...] * ATT_SCALE
    ck = jnp.concatenate([ck_ref[0, 0, hh] for hh in range(HEADS_PER_BLOCK)], axis=-1)
    cv = jnp.concatenate([cv_ref[0, 0, hh] for hh in range(HEADS_PER_BLOCK)], axis=-1)
    keys = jnp.concatenate([k_ref[...], ck], axis=0).astype(_bf16)
    vals = jnp.concatenate([v_ref[...], cv], axis=0).astype(_bf16)
    ctx_cols = slice(DEC_SEQ, DEC_SEQ + PAST_LEN)
    head_outs = []
    for hh, mask in enumerate(masks):
        e_ref = e_refs.at[hh]
        e_ref[:, :DEC_SEQ] = jnp.zeros((DEC_SEQ, DEC_SEQ), _bf16)
        q = jnp.where(mask, q32, 0.0).astype(_bf16)
        s_all = lax.dot_general(q, keys, _NT, preferred_element_type=_f32)
        denoms = []
        for p, (w0, n_rows, per_row) in enumerate(NA_PLAN):
            qrows = slice(p * NA_QBLK, (p + 1) * NA_QBLK)
            kcols = slice(w0 * GRID_W, (w0 + n_rows) * GRID_W)
            bias = jnp.concatenate([_na_pair_bias(bias_ref, hh, blocks, lane_lo) for blocks in per_row], axis=0)
            s_win = s_all[qrows, kcols] + bias
            s_ctx = s_all[qrows, ctx_cols]
            m = jnp.maximum(jnp.max(s_win, axis=-1, keepdims=True), jnp.max(s_ctx, axis=-1, keepdims=True))
            e_win = jnp.exp(s_win - m)
            e_ctx = jnp.exp(s_ctx - m)
            denoms.append(jnp.sum(e_win, axis=-1, keepdims=True) + jnp.sum(e_ctx, axis=-1, keepdims=True))
            e_ref[qrows, kcols] = e_win.astype(_bf16)
            e_ref[qrows, ctx_cols] = e_ctx.astype(_bf16)
        half = DEC_SEQ // 2
        o = jnp.concatenate([jnp.dot(e_ref[r0:r0 + half, :], vals, preferred_element_type=_f32)
                             for r0 in (0, half)], axis=0)
        head_outs.append(o / jnp.concatenate(denoms, axis=0))
    o_ref[...] = _select_heads(masks, head_outs).astype(_bf16)


NA_GRID = (N_HEAD_BLOCKS, DEC_BATCH)
NA_STEPS = NA_GRID[0] * NA_GRID[1]


def _na_attention(proj, cache_k, cache_v, bias_tab, layer, yb, cast_jobs):
    tile0 = N_CTX_TOK // DEC_SEQ
    qkv_spec = lambda lane_block: pl.BlockSpec((DEC_SEQ, LANES), lambda h, b: (tile0 + b, lane_block + h))
    cache_spec = pl.BlockSpec((1, 1, HEADS_PER_BLOCK, PAST_LEN, NA_HEAD_DIM), lambda h, b: (b, layer, h, 0, 0))
    cast_specs = [job.specs(lambda h, b: h * NA_GRID[1] + b) for job in cast_jobs]
    return pl.pallas_call(
        functools.partial(_na_attn_kernel, len(cast_jobs)),
        out_shape=(jax.ShapeDtypeStruct((N_TOK, NA_WIDTH), _bf16),) + tuple(job.out_shape for job in cast_jobs),
        grid=NA_GRID,
        in_specs=[
            qkv_spec(Q_LANE_BLOCK), qkv_spec(K_LANE_BLOCK), qkv_spec(V_LANE_BLOCK),
            cache_spec, cache_spec,
            pl.BlockSpec((1, 1, HEADS_PER_BLOCK, NA_TOEP_ROWS - 1, GRID_W, 2 * GRID_W),
                         lambda h, b: (layer, h, 0, 0, 0, 0)),
            pl.BlockSpec(memory_space=pl.ANY),
        ] + [s[0] for s in cast_specs],
        out_specs=(pl.BlockSpec((DEC_SEQ, LANES), lambda h, b: (tile0 + b, h)),) + tuple(s[1] for s in cast_specs),
        scratch_shapes=[pltpu.VMEM((HEADS_PER_BLOCK, DEC_SEQ, DEC_SEQ + PAST_LEN), _bf16)],
        input_output_aliases={6: 0},
        compiler_params=pltpu.CompilerParams(
            dimension_semantics=("arbitrary", "arbitrary"), vmem_limit_bytes=VMEM_LIMIT),
        name="na_attention",
    )(proj, proj, proj, cache_k, cache_v, bias_tab, yb, *[job.w for job in cast_jobs])


MIX_TM = 512
MIX_TC = 512
GA_COL_TILE = (2 * A_WIDTH + 3 * NA_WIDTH) // MIX_TC
GB_COL_TILE = GA_COL_TILE + D_MODEL // MIX_TC


def _mix_kernel(ga_ref, gb_ref, ya_ref, yb_ref, wpa_ref, wpb_ref, wout_ref, x_ref, gate_ref, g_ref, o_ref, ss_ref):
    j = pl.program_id(1)

    @pl.when(j == 0)
    def _():
        o_ref[...] = jnp.zeros_like(o_ref)

    pa = jnp.dot(ya_ref[...], wpa_ref[...], preferred_element_type=_f32)
    pb = jnp.dot(yb_ref[...], wpb_ref[...], preferred_element_type=_f32)
    merged = (_sigmoid(ga_ref[...]) * pa + _sigmoid(gb_ref[...]) * pb).astype(_bf16)
    o_ref[...] += jnp.dot(merged, wout_ref[...], preferred_element_type=_f32)

    @pl.when(j == pl.num_programs(1) - 1)
    def _():
        _residual_rmsnorm_inplace(o_ref, x_ref, ss_ref, MIX_TM, gate_ref, g_ref)


def _mix(proj, ya, yb, w_pa, w_pb, w_out, x, mod, g_post, layer):
    return pl.pallas_call(
        _mix_kernel,
        out_shape=jax.ShapeDtypeStruct((N_TOK, D_MODEL), _f32),
        grid=(N_TOK // MIX_TM, D_MODEL // MIX_TC),
        in_specs=[
            pl.BlockSpec((MIX_TM, MIX_TC), lambda i, j: (i, GA_COL_TILE + j)),
            pl.BlockSpec((MIX_TM, MIX_TC), lambda i, j: (i, GB_COL_TILE + j)),
            pl.BlockSpec((MIX_TM, A_WIDTH), lambda i, j: (i, 0)),
            pl.BlockSpec((MIX_TM, NA_WIDTH), lambda i, j: (i, 0)),
            pl.BlockSpec((A_WIDTH, MIX_TC), lambda i, j: (0, j)),
            pl.BlockSpec((NA_WIDTH, MIX_TC), lambda i, j: (0, j)),
            pl.BlockSpec((MIX_TC, D_MODEL), lambda i, j: (j, 0)),
            pl.BlockSpec((MIX_TM, D_MODEL), lambda i, j: (i, 0)),
            _mod_spec(layer, MIX_TM, 2),
            _layer_vec_spec(layer, D_MODEL),
        ],
        out_specs=pl.BlockSpec((MIX_TM, D_MODEL), lambda i, j: (i, 0)),
        scratch_shapes=[pltpu.VMEM((MIX_TM, LANES), _f32)],
        compiler_params=pltpu.CompilerParams(
            dimension_semantics=("arbitrary", "arbitrary"), vmem_limit_bytes=VMEM_LIMIT),
        name="mix_out",
    )(proj, proj, ya, yb, w_pa, w_pb, w_out, x, mod, g_post)


FFN_TM = 1024
FFN_TF = 512


FFN_OUT_SPLIT = 2


def _ffn_kernel(n_cast, x_ref, shift_ref, scale_ref, gate_ref, gpre_ref, gpost_ref, w1_ref, w2_ref, *rest):
    o_ref, h_ref, ss_ref = rest[n_cast], rest[-2], rest[-1]
    _run_cast_jobs(rest[:n_cast], rest[n_cast + 1:-2])
    k = pl.program_id(1)

    @pl.when(k == 0)
    def _():
        def zero_out(rows, cols):
            o_ref[rows, cols] = jnp.zeros((NORM_ROWS, LANES), _f32)
        _modnorm_to(x_ref, h_ref, ss_ref, FFN_TM, gpre_ref, scale_ref, shift_ref, also=zero_out)

    a = jnp.dot(h_ref[...], w1_ref[...], preferred_element_type=_f32)
    a = jnp.square(jnp.maximum(a, 0.0)).astype(_bf16)
    width = D_MODEL // FFN_OUT_SPLIT
    for s in range(FFN_OUT_SPLIT):
        cols = slice(s * width, (s + 1) * width)
        o_ref[:, cols] += jnp.dot(a, w2_ref[:, cols], preferred_element_type=_f32)

    @pl.when(k == pl.num_programs(1) - 1)
    def _():
        _residual_rmsnorm_inplace(o_ref, x_ref, ss_ref, FFN_TM, gate_ref, gpost_ref)


FFN_K_STEPS = D_FF // FFN_TF
FFN_CAST_TILES = 8
FFN_CAST_STEPS = FFN_CAST_TILES * FFN_K_STEPS


def _ffn_cast_step(i, k):
    return jnp.where(i < FFN_CAST_TILES, i * FFN_K_STEPS + k, FFN_CAST_STEPS - 1)


def _ffn(x, mod, g_pre, g_post, w1, w2, layer, tile0=0, n_tiles=N_TOK // FFN_TM, cast_jobs=()):
    assert not cast_jobs or n_tiles >= FFN_CAST_TILES
    cast_specs = [job.specs(_ffn_cast_step) for job in cast_jobs]
    outs = pl.pallas_call(
        functools.partial(_ffn_kernel, len(cast_jobs)),
        out_shape=(jax.ShapeDtypeStruct((n_tiles * FFN_TM, D_MODEL), _f32),)
        + tuple(job.out_shape for job in cast_jobs),
        grid=(n_tiles, FFN_K_STEPS),
        in_specs=[
            pl.BlockSpec((FFN_TM, D_MODEL), lambda i, k: (tile0 + i, 0)),
            _mod_spec(layer, FFN_TM, 3, tile0), _mod_spec(layer, FFN_TM, 4, tile0), _mod_spec(layer, FFN_TM, 5, tile0),
            _layer_vec_spec(layer, D_MODEL),
            _layer_vec_spec(layer, D_MODEL),
            pl.BlockSpec((D_MODEL, FFN_TF), lambda i, k: (0, k)),
            pl.BlockSpec((FFN_TF, D_MODEL), lambda i, k: (k, 0)),
        ] + [s[0] for s in cast_specs],
        out_specs=(pl.BlockSpec((FFN_TM, D_MODEL), lambda i, k: (i, 0)),) + tuple(s[1] for s in cast_specs),
        scratch_shapes=[pltpu.VMEM((FFN_TM, D_MODEL), _bf16), pltpu.VMEM((FFN_TM, LANES), _f32)],
        compiler_params=pltpu.CompilerParams(
            dimension_semantics=("arbitrary", "arbitrary"), vmem_limit_bytes=VMEM_LIMIT),
        name="ffn",
    )(x, mod, mod, mod, g_pre, g_post, w1, w2, *[job.w for job in cast_jobs])
    return outs


def kernel(x_prompt, x_sample, cache_ctx_k, cache_ctx_v, c, c_ctx, w_mod, b_mod, g_pre_mix, g_post_mix, g_pre_ffn,
           g_post_ffn, w_in, sgu_ln_g, sgu_ln_b, sgu_w, sgu_b, na_rpb, w_pa, w_pb, w_out, w_ff1, w_ff2):
    x = jnp.concatenate([x_prompt.reshape(N_CTX_TOK, D_MODEL), x_sample.reshape(DEC_BATCH * DEC_SEQ, D_MODEL)], axis=0)
    cvec = jnp.concatenate([c_ctx[None, :], c, jnp.zeros((MOD_ROWS - 1 - DEC_BATCH, D_MODEL), _f32)], axis=0)
    mod = _modulation(cvec, w_mod, b_mod).reshape(DEPTH, MOD_ROWS, 1, N_MOD * D_MODEL)
    bias = _na_bias_tables(na_rpb)
    cos_t, sin_t = _rope_tables()
    gdim = A_WIDTH // A_GROUPS
    sgu_b_lanes = jnp.broadcast_to(sgu_b[:, :, :, None], (DEPTH, A_GROUPS, CHUNK, gdim))

    vec = lambda a: a[:, None, :]
    g_pre_mix, g_post_mix, g_pre_ffn, g_post_ffn = vec(g_pre_mix), vec(g_post_mix), vec(g_pre_ffn), vec(g_post_ffn)
    sgu_ln_g, sgu_ln_b = vec(sgu_ln_g), vec(sgu_ln_b)
    sgu_w = sgu_w.astype(_bf16)

    states = ()
    for l in range(DEPTH):
        if l == 0:
            w_in_l = w_in
            ctx_jobs = [_CastJob(w_ff1, 0, CTX_STEPS, split_rows=False), _CastJob(w_ff2, 0, CTX_STEPS, split_rows=True)]
            na_jobs = [_CastJob(w, 0, NA_STEPS, split_rows=True) for w in (w_pa, w_pb, w_out)]
        else:
            w_in_l, ctx_jobs, na_jobs = w_in_b[None], [], []
        proj = _in_proj(x, mod, g_pre_mix, w_in_l, 0, cos_t, sin_t, l)
        ya = _sgu(proj, sgu_ln_g, sgu_ln_b, sgu_w, sgu_b_lanes, l)
        yb, k_state, v_state, *ctx_cast = _ctx_attention(proj, l, states, ctx_jobs)
        states = (k_state, v_state)
        yb, *na_cast = _na_attention(proj, cache_ctx_k, cache_ctx_v, bias, l, yb, na_jobs)
        if l == 0:
            (w_ff1_b, w_ff2_b), (w_pa_b, w_pb_b, w_out_b) = ctx_cast, na_cast
        x = _mix(proj, ya, yb, w_pa_b, w_pb_b, w_out_b, x, mod, g_post_mix, l)
        if l + 1 < DEPTH:
            nxt = l + 1
            jobs = [_CastJob(w, nxt, FFN_CAST_STEPS, split_rows=True) for w in (w_in, w_ff1, w_ff2, w_out)]
            jobs += [_CastJob(w, nxt, FFN_CAST_STEPS // 2, split_rows=True, steps_per_slab=2) for w in (w_pa, w_pb)]
            x, w_in_b, w_ff1_b, w_ff2_b, w_out_b, w_pa_b, w_pb_b = _ffn(
                x, mod, g_pre_ffn, g_post_ffn, w_ff1_b, w_ff2_b, l, cast_jobs=jobs)
        else:
            n_ctx_tiles = N_CTX_TOK // FFN_TM
            last = functools.partial(_ffn, x, mod, g_pre_ffn, g_post_ffn, w_ff1_b, w_ff2_b, l)
            y_prompt, = last(tile0=0, n_tiles=n_ctx_tiles)
            y_sample, = last(tile0=n_ctx_tiles, n_tiles=N_TOK // FFN_TM - n_ctx_tiles)
    return (y_prompt.reshape(BATCH, SEQ, D_MODEL), y_sample.reshape(DEC_BATCH, DEC_SEQ, D_MODEL), states[0], states[1])
```

```python
import functools

import numpy as np
import jax
import jax.numpy as jnp
from jax import lax
from jax.experimental import pallas as pl
from jax.experimental.pallas import tpu as pltpu

D_MODEL = 2048
BATCH = 32
SEQ = 256
DEPTH = 4
DEC_BATCH = 2
DEC_SEQ = 1024
PAST_LEN = 512
GRID_W = 64
CHUNK = 128
A_WIDTH = 1024
A_GROUPS = 8
NA_HEADS = 16
NA_HEAD_DIM = 64
NA_WIDTH = NA_HEADS * NA_HEAD_DIM
NA_KH_MAX = 8
NA_KW = 16
D_FF = 4 * D_MODEL
ROPE_THETA = 10000.0
EPS = 1e-6
N_MOD = 6
IN_COLS = 2 * A_WIDTH + 3 * NA_WIDTH + 2 * D_MODEL

N_CTX_TOK = BATCH * SEQ
N_TOK = N_CTX_TOK + DEC_BATCH * DEC_SEQ
ROWS = DEC_SEQ // GRID_W
NA_KH = min(NA_KH_MAX, ROWS)
MOD_ROWS = 8
LANES = 128
HEADS_PER_BLOCK = LANES // NA_HEAD_DIM
N_HEAD_BLOCKS = NA_HEADS // HEADS_PER_BLOCK

NA_QROWS = 2
NA_QBLK = NA_QROWS * GRID_W
NA_NPAIR = ROWS // NA_QROWS
NA_TOEP_ROWS = 2 * NA_KH_MAX - 1

VMEM_LIMIT = 56 * 1024 * 1024

_f32 = jnp.float32
_bf16 = jnp.bfloat16
_NT = (((1,), (1,)), ((), ()))


def _na_window_plan():
    plan = []
    for p in range(NA_NPAIR):
        rows = [NA_QROWS * p + rr for rr in range(NA_QROWS)]
        starts = [int(np.clip(r - NA_KH // 2, 0, ROWS - NA_KH)) for r in rows]
        w0 = min(starts) // 2 * 2
        w1 = -(-(max(starts) + NA_KH) // 2) * 2
        per_row = []
        for r, rs in zip(rows, starts):
            blocks = []
            for i in range(w0, w1, 2):
                keep_lo = rs <= i < rs + NA_KH
                keep_hi = rs <= i + 1 < rs + NA_KH
                a = i - r + NA_KH_MAX - 1
                assert not (keep_lo or keep_hi) or 0 <= a <= NA_TOEP_ROWS - 2
                blocks.append((a, keep_lo, keep_hi))
            per_row.append(blocks)
        plan.append((w0, w1 - w0, per_row))
    return plan


NA_PLAN = _na_window_plan()


def _mod_row(tile, tm):
    return jnp.maximum((tile * tm - N_CTX_TOK) // DEC_SEQ + 1, 0)


def _sigmoid(x):
    return 1.0 / (1.0 + jnp.exp(-x))


def _gelu(x):
    return 0.5 * x * (1.0 + jnp.tanh(np.sqrt(2.0 / np.pi).astype(np.float32) * (x + 0.044715 * (x * x * x))))


SUMSQ_ROWS = 256
NORM_ROWS = 64
N_LANE_BLOCKS = D_MODEL // LANES


def _row_sumsq_pass(src_ref, ss_ref, n_rows):
    def body(c, carry):
        rows = pl.ds(pl.multiple_of(c * SUMSQ_ROWS, SUMSQ_ROWS), SUMSQ_ROWS)
        acc = jnp.zeros((SUMSQ_ROWS, LANES), _f32)
        for cb in range(N_LANE_BLOCKS):
            v = src_ref[rows, cb * LANES:(cb + 1) * LANES]
            acc = acc + v * v
        ss_ref[rows, :] = jnp.broadcast_to(jnp.sum(acc, axis=-1, keepdims=True), (SUMSQ_ROWS, LANES))
        return carry
    lax.fori_loop(0, n_rows // SUMSQ_ROWS, body, 0)


def _row_rescale_pass(ss_ref, n_rows, block_fn):
    def body(c, carry):
        rows = pl.ds(pl.multiple_of(c * NORM_ROWS, NORM_ROWS), NORM_ROWS)
        r = lax.rsqrt(ss_ref[rows, :] * (1.0 / D_MODEL) + EPS)
        for cb in range(N_LANE_BLOCKS):
            block_fn(rows, slice(cb * LANES, (cb + 1) * LANES), r)
        return carry
    lax.fori_loop(0, n_rows // NORM_ROWS, body, 0)


def _modnorm_to(x_ref, h_ref, ss_ref, n_rows, g_ref, scale_ref, shift_ref, also=None):
    _row_sumsq_pass(x_ref, ss_ref, n_rows)

    def block(rows, cols, r):
        y = x_ref[rows, cols] * r
        h = (y * g_ref[0, :, cols]) * (1.0 + scale_ref[0, 0, :, cols]) + shift_ref[0, 0, :, cols]
        h_ref[rows, cols] = h.astype(_bf16)
        if also is not None:
            also(rows, cols)
    _row_rescale_pass(ss_ref, n_rows, block)


def _residual_rmsnorm_inplace(o_ref, x_ref, ss_ref, n_rows, gate_ref, g_ref):
    _row_sumsq_pass(o_ref, ss_ref, n_rows)

    def block(rows, cols, r):
        y = (o_ref[rows, cols] * r) * g_ref[0, :, cols]
        o_ref[rows, cols] = x_ref[rows, cols] + gate_ref[0, 0, :, cols] * y
    _row_rescale_pass(ss_ref, n_rows, block)


def _mod_kernel(c_ref, w_ref, b_ref, o_ref):
    c = c_ref[...]
    s = (c * _sigmoid(c)).astype(_bf16)
    o_ref[0] = jnp.dot(s, w_ref[0].astype(_bf16), preferred_element_type=_f32) + b_ref[0]


def _modulation(cvec, w_mod, b_mod):
    tn = 1024
    n_cols = N_MOD * D_MODEL
    return pl.pallas_call(
        _mod_kernel,
        out_shape=jax.ShapeDtypeStruct((DEPTH, MOD_ROWS, n_cols), _f32),
        grid=(DEPTH, n_cols // tn),
        in_specs=[
            pl.BlockSpec((MOD_ROWS, D_MODEL), lambda l, j: (0, 0)),
            pl.BlockSpec((1, D_MODEL, tn), lambda l, j: (l, 0, j)),
            pl.BlockSpec((1, 1, tn), lambda l, j: (l, 0, j)),
        ],
        out_specs=pl.BlockSpec((1, MOD_ROWS, tn), lambda l, j: (l, 0, j)),
        compiler_params=pltpu.CompilerParams(
            dimension_semantics=("arbitrary", "arbitrary"), vmem_limit_bytes=VMEM_LIMIT),
        name="modulation",
    )(cvec, w_mod, b_mod.reshape(DEPTH, 1, n_cols))


def _toeplitz_kernel(rpb_ref, onehot_ref, mask_ref, o_ref):
    o_ref[...] = jnp.dot(rpb_ref[...], onehot_ref[...], preferred_element_type=_f32,
                         precision=lax.Precision.HIGHEST) + mask_ref[...]


def _na_bias_tables(na_rpb):
    n_pair = NA_TOEP_ROWS - 1
    n_col = 2 * NA_KW - 1
    n_col_pad = 32
    cols = np.arange(GRID_W)
    cstart = np.clip(cols - NA_KW // 2, 0, GRID_W - NA_KW)
    in_win = (cols[None, :] >= cstart[:, None]) & (cols[None, :] < cstart[:, None] + NA_KW)
    col_idx = np.clip(cols[None, :] - cols[:, None] + NA_KW - 1, 0, n_col - 1)
    onehot = np.zeros((2, n_col_pad, GRID_W, 2, GRID_W), np.float32)
    for half in range(2):
        onehot[half, :, :, half, :] = (col_idx[None] == np.arange(n_col_pad)[:, None, None]) & in_win[None]
    onehot = onehot.reshape(2 * n_col_pad, 2 * GRID_W * GRID_W)
    mask = np.where(np.broadcast_to(in_win[:, None, :], (GRID_W, 2, GRID_W)), 0.0, -np.inf)
    mask = mask.reshape(1, -1).astype(np.float32)
    rpb_pad = jnp.pad(na_rpb, ((0, 0), (0, 0), (0, 0), (0, n_col_pad - n_col)))
    rpb_pairs = jnp.concatenate([rpb_pad[:, :, :-1], rpb_pad[:, :, 1:]], axis=-1)
    n_rows = DEPTH * NA_HEADS * n_pair
    row_block = 128
    pairs = pl.pallas_call(
        _toeplitz_kernel,
        out_shape=jax.ShapeDtypeStruct((n_rows, onehot.shape[1]), _f32),
        grid=(n_rows // row_block,),
        in_specs=[
            pl.BlockSpec((row_block, 2 * n_col_pad), lambda r: (r, 0)),
            pl.BlockSpec(onehot.shape, lambda r: (0, 0)),
            pl.BlockSpec(mask.shape, lambda r: (0, 0)),
        ],
        out_specs=pl.BlockSpec((row_block, onehot.shape[1]), lambda r: (r, 0)),
        compiler_params=pltpu.CompilerParams(dimension_semantics=("arbitrary",), vmem_limit_bytes=VMEM_LIMIT),
        name="rpb_toeplitz",
    )(rpb_pairs.reshape(n_rows, 2 * n_col_pad), jnp.asarray(onehot), jnp.asarray(mask))
    return pairs.reshape(DEPTH, N_HEAD_BLOCKS, HEADS_PER_BLOCK, n_pair, GRID_W, 2 * GRID_W)


def _rope_tables():
    t = jnp.arange(DEC_SEQ)
    half = NA_HEAD_DIM // 4
    freqs = ROPE_THETA ** (-jnp.arange(half, dtype=_f32) / half)
    ang_r = (t // GRID_W).astype(_f32)[:, None] * freqs[None, :]
    ang_c = (t % GRID_W).astype(_f32)[:, None] * freqs[None, :]
    cos_h = jnp.concatenate([jnp.cos(ang_r)] * 2 + [jnp.cos(ang_c)] * 2, axis=-1)
    sin_h = jnp.concatenate([-jnp.sin(ang_r), jnp.sin(ang_r), -jnp.sin(ang_c), jnp.sin(ang_c)], axis=-1)
    return jnp.tile(cos_h, (1, HEADS_PER_BLOCK)), jnp.tile(sin_h, (1, HEADS_PER_BLOCK))


IN_TM = 1024
IN_TN_F32_WEIGHT = 1024
IN_TN_BF16_WEIGHT = 1536
IN_PROJ_COLS = 2 * A_WIDTH + 3 * NA_WIDTH + 2 * D_MODEL
ROPE_COLS = (2 * A_WIDTH, 2 * A_WIDTH + 2 * NA_WIDTH)


def _in_kernel(tn, x_ref, shift_ref, scale_ref, g_ref, w_ref, cos_ref, sin_ref, o_ref, h_ref, ss_ref):
    i = pl.program_id(0)
    j = pl.program_id(1)

    @pl.when(j == 0)
    def _():
        _modnorm_to(x_ref, h_ref, ss_ref, IN_TM, g_ref, scale_ref, shift_ref)

    o_ref[...] = jnp.dot(h_ref[...], w_ref[0].astype(_bf16), preferred_element_type=_f32)

    for jt in range(IN_PROJ_COLS // tn):
        blocks = [cb for cb in range(tn // LANES) if ROPE_COLS[0] <= jt * tn + cb * LANES < ROPE_COLS[1]]
        if not blocks:
            continue

        @pl.when(jnp.logical_and(i >= N_CTX_TOK // IN_TM, j == jt))
        def _(blocks=blocks):
            lane = lax.broadcasted_iota(jnp.int32, (IN_TM, LANES), 1)
            first_half = (lane % (NA_HEAD_DIM // 2)) < (NA_HEAD_DIM // 4)
            cos = cos_ref[...]
            sin = sin_ref[...]
            for cb in blocks:
                cols = slice(cb * LANES, (cb + 1) * LANES)
                xb = o_ref[:, cols]
                partner = jnp.where(first_half, pltpu.roll(xb, LANES - NA_HEAD_DIM // 4, axis=1),
                                    pltpu.roll(xb, NA_HEAD_DIM // 4, axis=1))
                o_ref[:, cols] = xb * cos + partner * sin


def _mod_spec(layer, tm, chunk, tile0=0):
    return pl.BlockSpec((1, 1, 1, D_MODEL), lambda i, j: (layer, _mod_row(tile0 + i, tm), 0, chunk))


def _layer_vec_spec(layer, width):
    return pl.BlockSpec((1, 1, width), lambda *_: (layer, 0, 0))


def _in_proj(x, mod, g, w_in, w_layer, cos_t, sin_t, layer):
    n_m = N_TOK // IN_TM
    tn = IN_TN_BF16_WEIGHT if w_in.dtype == _bf16 else IN_TN_F32_WEIGHT
    return pl.pallas_call(
        functools.partial(_in_kernel, tn),
        out_shape=jax.ShapeDtypeStruct((N_TOK, IN_PROJ_COLS), _f32),
        grid=(n_m, IN_PROJ_COLS // tn),
        in_specs=[
            pl.BlockSpec((IN_TM, D_MODEL), lambda i, j: (i, 0)),
            _mod_spec(layer, IN_TM, 0), _mod_spec(layer, IN_TM, 1),
            _layer_vec_spec(layer, D_MODEL),
            pl.BlockSpec((1, D_MODEL, tn), lambda i, j: (w_layer, 0, j)),
            pl.BlockSpec((DEC_SEQ, LANES), lambda i, j: (0, 0)),
            pl.BlockSpec((DEC_SEQ, LANES), lambda i, j: (0, 0)),
        ],
        out_specs=pl.BlockSpec((IN_TM, tn), lambda i, j: (i, j)),
        scratch_shapes=[pltpu.VMEM((IN_TM, D_MODEL), _bf16), pltpu.VMEM((IN_TM, LANES), _f32)],
        compiler_params=pltpu.CompilerParams(
            dimension_semantics=("arbitrary", "arbitrary"), vmem_limit_bytes=VMEM_LIMIT),
        name="in_proj",
    )(x, mod, mod, g, w_in, cos_t, sin_t)


SGU_TM = 1024


def _sgu_kernel(au_ref, av_ref, lng_ref, lnb_ref, ws_ref, bs_ref, o_ref):
    gdim = A_WIDTH // A_GROUPS

    def body(c, carry):
        rows = pl.ds(pl.multiple_of(c * CHUNK, CHUNK), CHUNK)
        u = _gelu(au_ref[rows, :])
        gv = _gelu(av_ref[rows, :])
        mu = jnp.mean(gv, axis=-1, keepdims=True)
        var = jnp.mean(jnp.square(gv - mu), axis=-1, keepdims=True)
        vn = ((gv - mu) * lax.rsqrt(var + EPS) * lng_ref[0] + lnb_ref[0]).astype(_bf16)
        for g in range(A_GROUPS):
            cols = slice(g * gdim, (g + 1) * gdim)
            s = jnp.dot(ws_ref[0, g], vn[:, cols], preferred_element_type=_f32) + bs_ref[0, g]
            o_ref[rows, cols] = (u[:, cols] * s).astype(_bf16)
        return carry

    lax.fori_loop(0, SGU_TM // CHUNK, body, 0)


def _sgu(proj, ln_g, ln_b, w_s, b_s, layer):
    gdim = A_WIDTH // A_GROUPS
    return pl.pallas_call(
        _sgu_kernel,
        out_shape=jax.ShapeDtypeStruct((N_TOK, A_WIDTH), _bf16),
        grid=(N_TOK // SGU_TM,),
        in_specs=[
            pl.BlockSpec((SGU_TM, A_WIDTH), lambda i: (i, 0)),
            pl.BlockSpec((SGU_TM, A_WIDTH), lambda i: (i, 1)),
            _layer_vec_spec(layer, A_WIDTH),
            _layer_vec_spec(layer, A_WIDTH),
            pl.BlockSpec((1, A_GROUPS, CHUNK, CHUNK), lambda i: (layer, 0, 0, 0)),
            pl.BlockSpec((1, A_GROUPS, CHUNK, gdim), lambda i: (layer, 0, 0, 0)),
        ],
        out_specs=pl.BlockSpec((SGU_TM, A_WIDTH), lambda i: (i, 0)),
        compiler_params=pltpu.CompilerParams(dimension_semantics=("arbitrary",), vmem_limit_bytes=VMEM_LIMIT),
        name="sgu",
    )(proj, proj, ln_g, ln_b, w_s, b_s)


ATT_SCALE = NA_HEAD_DIM ** -0.5
Q_LANE_BLOCK = (2 * A_WIDTH) // LANES
K_LANE_BLOCK = (2 * A_WIDTH + NA_WIDTH) // LANES
V_LANE_BLOCK = (2 * A_WIDTH + 2 * NA_WIDTH) // LANES


def _softmax_pv(parts):
    m = functools.reduce(jnp.maximum, [jnp.max(s, axis=-1, keepdims=True) for s, _ in parts])
    es = [jnp.exp(s - m) for s, _ in parts]
    l = functools.reduce(jnp.add, [jnp.sum(e, axis=-1, keepdims=True) for e in es])
    o = functools.reduce(jnp.add, [jnp.dot(e.astype(_bf16), v, preferred_element_type=_f32)
                                   for e, (_, v) in zip(es, parts)])
    return o / l


CTX_SEQS_PER_STEP = 4
CTX_TM = CTX_SEQS_PER_STEP * SEQ


def _head_lane_masks(rows):
    lane = lax.broadcasted_iota(jnp.int32, (rows, LANES), 1)
    return [jnp.logical_and(lane >= hh * NA_HEAD_DIM, lane < (hh + 1) * NA_HEAD_DIM)
            for hh in range(HEADS_PER_BLOCK)]


def _select_heads(masks, per_head):
    out = per_head[-1]
    for mask, val in zip(masks[:-1], per_head[:-1]):
        out = jnp.where(mask, val, out)
    return out


class _CastJob:
    def __init__(self, w, layer, n_slabs, split_rows, steps_per_slab=1):
        _, r, c = w.shape
        self.w, self.layer = w, layer
        self.out_shape = jax.ShapeDtypeStruct((r, c), _bf16)
        self.block = (r // n_slabs, c) if split_rows else (r, c // n_slabs)
        self.split_rows = split_rows
        self.steps_per_slab = steps_per_slab

    def specs(self, step_of):
        slab_of = lambda *g: step_of(*g) // self.steps_per_slab
        pos = (lambda *g: (slab_of(*g), 0)) if self.split_rows else (lambda *g: (0, slab_of(*g)))
        layer = self.layer
        return (pl.BlockSpec((1,) + self.block, lambda *g: (layer,) + pos(*g)), pl.BlockSpec(self.block, pos))


def _run_cast_jobs(src_refs, dst_refs):
    for src, dst in zip(src_refs, dst_refs):
        dst[...] = src[0].astype(_bf16)


def _ctx_attn_kernel(n_cast, q_ref, k_ref, v_ref, *rest):
    outs = rest[len(rest) - 3 - n_cast:]
    o_ref, kst_ref, vst_ref = outs[:3]
    _run_cast_jobs(rest[:n_cast], outs[3:])
    masks = _head_lane_masks(SEQ)
    seq_outs = []
    for s in range(CTX_SEQS_PER_STEP):
        rows = slice(s * SEQ, (s + 1) * SEQ)
        q32 = q_ref[rows, :] * ATT_SCALE
        k32 = k_ref[rows, :]
        v32 = v_ref[rows, :]
        for hh in range(HEADS_PER_BLOCK):
            cols = slice(hh * NA_HEAD_DIM, (hh + 1) * NA_HEAD_DIM)
            kst_ref[s, 0, hh] = k32[:, cols]
            vst_ref[s, 0, hh] = v32[:, cols]
        kb = k32.astype(_bf16)
        vb = v32.astype(_bf16)
        per_head = []
        for mask in masks:
            qh = jnp.where(mask, q32, 0.0).astype(_bf16)
            sc = lax.dot_general(qh, kb, _NT, preferred_element_type=_f32)
            per_head.append(_softmax_pv([(sc, vb)]))
        seq_outs.append(_select_heads(masks, per_head))
    o_ref[...] = jnp.concatenate(seq_outs, axis=0).astype(_bf16)


CTX_GRID = (N_CTX_TOK // CTX_TM, N_HEAD_BLOCKS)
CTX_STEPS = CTX_GRID[0] * CTX_GRID[1]


def _ctx_attention(proj, layer, states, cast_jobs):
    state_shape = jax.ShapeDtypeStruct((BATCH, DEPTH, NA_HEADS, SEQ, NA_HEAD_DIM), _f32)
    state_spec = pl.BlockSpec((CTX_SEQS_PER_STEP, 1, HEADS_PER_BLOCK, SEQ, NA_HEAD_DIM),
                              lambda i, h: (i, layer, h, 0, 0))
    cast_specs = [job.specs(lambda i, h: i * CTX_GRID[1] + h) for job in cast_jobs]
    n_cast, n_state_in = len(cast_jobs), len(states)
    return pl.pallas_call(
        functools.partial(_ctx_attn_kernel, n_cast),
        out_shape=(jax.ShapeDtypeStruct((N_TOK, NA_WIDTH), _bf16), state_shape, state_shape)
        + tuple(job.out_shape for job in cast_jobs),
        grid=CTX_GRID,
        in_specs=[
            pl.BlockSpec((CTX_TM, LANES), lambda i, h: (i, Q_LANE_BLOCK + h)),
            pl.BlockSpec((CTX_TM, LANES), lambda i, h: (i, K_LANE_BLOCK + h)),
            pl.BlockSpec((CTX_TM, LANES), lambda i, h: (i, V_LANE_BLOCK + h)),
        ] + [s[0] for s in cast_specs] + [pl.BlockSpec(memory_space=pl.ANY)] * n_state_in,
        out_specs=(pl.BlockSpec((CTX_TM, LANES), lambda i, h: (i, h)), state_spec, state_spec)
        + tuple(s[1] for s in cast_specs),
        input_output_aliases={3 + n_cast + n: 1 + n for n in range(n_state_in)},
        compiler_params=pltpu.CompilerParams(
            dimension_semantics=("arbitrary", "arbitrary"), vmem_limit_bytes=VMEM_LIMIT),
        name="ctx_attention",
    )(proj, proj, proj, *[job.w for job in cast_jobs], *states)


def _na_pair_bias(bias_ref, hh, blocks, lane_lo):
    pieces = []
    for a, keep_lo, keep_hi in blocks:
        if not (keep_lo or keep_hi):
            pieces.append(jnp.full((GRID_W, 2 * GRID_W), -jnp.inf, _f32))
            continue
        piece = bias_ref[0, 0, hh, a]
        if not keep_lo:
            piece = jnp.where(lane_lo, -jnp.inf, piece)
        if not keep_hi:
            piece = jnp.where(lane_lo, piece, -jnp.inf)
        pieces.append(piece)
    return jnp.concatenate(pieces, axis=-1)


def _na_attn_kernel(n_cast, q_ref, k_ref, v_ref, ck_ref, cv_ref, bias_ref, yb_hbm_ref, *rest):
    del yb_hbm_ref
    o_ref, e_refs = rest[n_cast], rest[-1]
    _run_cast_jobs(rest[:n_cast], rest[n_cast + 1:-1])
    lane_lo = lax.broadcasted_iota(jnp.int32, (GRID_W, 2 * GRID_W), 1) < GRID_W
    masks = _head_lane_masks(DEC_SEQ)
    q32 = q_ref[...] * ATT_SCALE
    ck = jnp.concatenate([ck_ref[0, 0, hh] for hh in range(HEADS_PER_BLOCK)], axis=-1)
    cv = jnp.concatenate([cv_ref[0, 0, hh] for hh in range(HEADS_PER_BLOCK)], axis=-1)
    keys = jnp.concatenate([k_ref[...], ck], axis=0).astype(_bf16)
    vals = jnp.concatenate([v_ref[...], cv], axis=0).astype(_bf16)
    ctx_cols = slice(DEC_SEQ, DEC_SEQ + PAST_LEN)
    head_outs = []
    for hh, mask in enumerate(masks):
        e_ref = e_refs.at[hh]
        e_ref[:, :DEC_SEQ] = jnp.zeros((DEC_SEQ, DEC_SEQ), _bf16)
        q = jnp.where(mask, q32, 0.0).astype(_bf16)
        s_all = lax.dot_general(q, keys, _NT, preferred_element_type=_f32)
        denoms = []
        for p, (w0, n_rows, per_row) in enumerate(NA_PLAN):
            qrows = slice(p * NA_QBLK, (p + 1) * NA_QBLK)
            kcols = slice(w0 * GRID_W, (w0 + n_rows) * GRID_W)
            bias = jnp.concatenate([_na_pair_bias(bias_ref, hh, blocks, lane_lo) for blocks in per_row], axis=0)
            s_win = s_all[qrows, kcols] + bias
            s_ctx = s_all[qrows, ctx_cols]
            m = jnp.maximum(jnp.max(s_win, axis=-1, keepdims=True), jnp.max(s_ctx, axis=-1, keepdims=True))
            e_win = jnp.exp(s_win - m)
            e_ctx = jnp.exp(s_ctx - m)
            denoms.append(jnp.sum(e_win, axis=-1, keepdims=True) + jnp.sum(e_ctx, axis=-1, keepdims=True))
            e_ref[qrows, kcols] = e_win.astype(_bf16)
            e_ref[qrows, ctx_cols] = e_ctx.astype(_bf16)
        half = DEC_SEQ // 2
        o = jnp.concatenate([jnp.dot(e_ref[r0:r0 + half, :], vals, preferred_element_type=_f32)
                             for r0 in (0, half)], axis=0)
        head_outs.append(o / jnp.concatenate(denoms, axis=0))
    o_ref[...] = _select_heads(masks, head_outs).astype(_bf16)


NA_GRID = (N_HEAD_BLOCKS, DEC_BATCH)
NA_STEPS = NA_GRID[0] * NA_GRID[1]


def _na_attention(proj, cache_k, cache_v, bias_tab, layer, yb, cast_jobs):
    tile0 = N_CTX_TOK // DEC_SEQ
    qkv_spec = lambda lane_block: pl.BlockSpec((DEC_SEQ, LANES), lambda h, b: (tile0 + b, lane_block + h))
    cache_spec = pl.BlockSpec((1, 1, HEADS_PER_BLOCK, PAST_LEN, NA_HEAD_DIM), lambda h, b: (b, layer, h, 0, 0))
    cast_specs = [job.specs(lambda h, b: h * NA_GRID[1] + b) for job in cast_jobs]
    return pl.pallas_call(
        functools.partial(_na_attn_kernel, len(cast_jobs)),
        out_shape=(jax.ShapeDtypeStruct((N_TOK, NA_WIDTH), _bf16),) + tuple(job.out_shape for job in cast_jobs),
        grid=NA_GRID,
        in_specs=[
            qkv_spec(Q_LANE_BLOCK), qkv_spec(K_LANE_BLOCK), qkv_spec(V_LANE_BLOCK),
            cache_spec, cache_spec,
            pl.BlockSpec((1, 1, HEADS_PER_BLOCK, NA_TOEP_ROWS - 1, GRID_W, 2 * GRID_W),
                         lambda h, b: (layer, h, 0, 0, 0, 0)),
            pl.BlockSpec(memory_space=pl.ANY),
        ] + [s[0] for s in cast_specs],
        out_specs=(pl.BlockSpec((DEC_SEQ, LANES), lambda h, b: (tile0 + b, h)),) + tuple(s[1] for s in cast_specs),
        scratch_shapes=[pltpu.VMEM((HEADS_PER_BLOCK, DEC_SEQ, DEC_SEQ + PAST_LEN), _bf16)],
        input_output_aliases={6: 0},
        compiler_params=pltpu.CompilerParams(
            dimension_semantics=("arbitrary", "arbitrary"), vmem_limit_bytes=VMEM_LIMIT),
        name="na_attention",
    )(proj, proj, proj, cache_k, cache_v, bias_tab, yb, *[job.w for job in cast_jobs])


MIX_TM = 512
MIX_TC = 1024
GA_COL_TILE = (2 * A_WIDTH + 3 * NA_WIDTH) // MIX_TC
GB_COL_TILE = GA_COL_TILE + D_MODEL // MIX_TC


def _mix_kernel(ga_ref, gb_ref, ya_ref, yb_ref, wpa_ref, wpb_ref, wout_ref, x_ref, gate_ref, g_ref, o_ref, ss_ref):
    j = pl.program_id(1)

    @pl.when(j == 0)
    def _():
        o_ref[...] = jnp.zeros_like(o_ref)

    pa = jnp.dot(ya_ref[...], wpa_ref[...], preferred_element_type=_f32)
    pb = jnp.dot(yb_ref[...], wpb_ref[...], preferred_element_type=_f32)
    merged = (_sigmoid(ga_ref[...]) * pa + _sigmoid(gb_ref[...]) * pb).astype(_bf16)
    o_ref[...] += jnp.dot(merged, wout_ref[...], preferred_element_type=_f32)

    @pl.when(j == pl.num_programs(1) - 1)
    def _():
        _residual_rmsnorm_inplace(o_ref, x_ref, ss_ref, MIX_TM, gate_ref, g_ref)


def _mix(proj, ya, yb, w_pa, w_pb, w_out, x, mod, g_post, layer):
    return pl.pallas_call(
        _mix_kernel,
        out_shape=jax.ShapeDtypeStruct((N_TOK, D_MODEL), _f32),
        grid=(N_TOK // MIX_TM, D_MODEL // MIX_TC),
        in_specs=[
            pl.BlockSpec((MIX_TM, MIX_TC), lambda i, j: (i, GA_COL_TILE + j)),
            pl.BlockSpec((MIX_TM, MIX_TC), lambda i, j: (i, GB_COL_TILE + j)),
            pl.BlockSpec((MIX_TM, A_WIDTH), lambda i, j: (i, 0)),
            pl.BlockSpec((MIX_TM, NA_WIDTH), lambda i, j: (i, 0)),
            pl.BlockSpec((A_WIDTH, MIX_TC), lambda i, j: (0, j)),
            pl.BlockSpec((NA_WIDTH, MIX_TC), lambda i, j: (0, j)),
            pl.BlockSpec((MIX_TC, D_MODEL), lambda i, j: (j, 0)),
            pl.BlockSpec((MIX_TM, D_MODEL), lambda i, j: (i, 0)),
            _mod_spec(layer, MIX_TM, 2),
            _layer_vec_spec(layer, D_MODEL),
        ],
        out_specs=pl.BlockSpec((MIX_TM, D_MODEL), lambda i, j: (i, 0)),
        scratch_shapes=[pltpu.VMEM((MIX_TM, LANES), _f32)],
        compiler_params=pltpu.CompilerParams(
            dimension_semantics=("arbitrary", "arbitrary"), vmem_limit_bytes=VMEM_LIMIT),
        name="mix_out",
    )(proj, proj, ya, yb, w_pa, w_pb, w_out, x, mod, g_post)


FFN_TM = 1024
FFN_TF = 512


FFN_OUT_SPLIT = 2


def _ffn_kernel(n_cast, x_ref, shift_ref, scale_ref, gate_ref, gpre_ref, gpost_ref, w1_ref, w2_ref, *rest):
    o_ref, h_ref, ss_ref = rest[n_cast], rest[-2], rest[-1]
    _run_cast_jobs(rest[:n_cast], rest[n_cast + 1:-2])
    k = pl.program_id(1)

    @pl.when(k == 0)
    def _():
        def zero_out(rows, cols):
            o_ref[rows, cols] = jnp.zeros((NORM_ROWS, LANES), _f32)
        _modnorm_to(x_ref, h_ref, ss_ref, FFN_TM, gpre_ref, scale_ref, shift_ref, also=zero_out)

    a = jnp.dot(h_ref[...], w1_ref[...], preferred_element_type=_f32)
    a = jnp.square(jnp.maximum(a, 0.0)).astype(_bf16)
    width = D_MODEL // FFN_OUT_SPLIT
    for s in range(FFN_OUT_SPLIT):
        cols = slice(s * width, (s + 1) * width)
        o_ref[:, cols] += jnp.dot(a, w2_ref[:, cols], preferred_element_type=_f32)

    @pl.when(k == pl.num_programs(1) - 1)
    def _():
        _residual_rmsnorm_inplace(o_ref, x_ref, ss_ref, FFN_TM, gate_ref, gpost_ref)


FFN_K_STEPS = D_FF // FFN_TF
FFN_CAST_TILES = 8
FFN_CAST_STEPS = FFN_CAST_TILES * FFN_K_STEPS


def _ffn_cast_step(i, k):
    return jnp.where(i < FFN_CAST_TILES, i * FFN_K_STEPS + k, FFN_CAST_STEPS - 1)


def _ffn(x, mod, g_pre, g_post, w1, w2, layer, tile0=0, n_tiles=N_TOK // FFN_TM, cast_jobs=()):
    assert not cast_jobs or n_tiles >= FFN_CAST_TILES
    cast_specs = [job.specs(_ffn_cast_step) for job in cast_jobs]
    outs = pl.pallas_call(
        functools.partial(_ffn_kernel, len(cast_jobs)),
        out_shape=(jax.ShapeDtypeStruct((n_tiles * FFN_TM, D_MODEL), _f32),)
        + tuple(job.out_shape for job in cast_jobs),
        grid=(n_tiles, FFN_K_STEPS),
        in_specs=[
            pl.BlockSpec((FFN_TM, D_MODEL), lambda i, k: (tile0 + i, 0)),
            _mod_spec(layer, FFN_TM, 3, tile0), _mod_spec(layer, FFN_TM, 4, tile0), _mod_spec(layer, FFN_TM, 5, tile0),
            _layer_vec_spec(layer, D_MODEL),
            _layer_vec_spec(layer, D_MODEL),
            pl.BlockSpec((D_MODEL, FFN_TF), lambda i, k: (0, k)),
            pl.BlockSpec((FFN_TF, D_MODEL), lambda i, k: (k, 0)),
        ] + [s[0] for s in cast_specs],
        out_specs=(pl.BlockSpec((FFN_TM, D_MODEL), lambda i, k: (i, 0)),) + tuple(s[1] for s in cast_specs),
        scratch_shapes=[pltpu.VMEM((FFN_TM, D_MODEL), _bf16), pltpu.VMEM((FFN_TM, LANES), _f32)],
        compiler_params=pltpu.CompilerParams(
            dimension_semantics=("arbitrary", "arbitrary"), vmem_limit_bytes=VMEM_LIMIT),
        name="ffn",
    )(x, mod, mod, mod, g_pre, g_post, w1, w2, *[job.w for job in cast_jobs])
    return outs


def kernel(x_prompt, x_sample, cache_ctx_k, cache_ctx_v, c, c_ctx, w_mod, b_mod, g_pre_mix, g_post_mix, g_pre_ffn,
           g_post_ffn, w_in, sgu_ln_g, sgu_ln_b, sgu_w, sgu_b, na_rpb, w_pa, w_pb, w_out, w_ff1, w_ff2):
    x = jnp.concatenate([x_prompt.reshape(N_CTX_TOK, D_MODEL), x_sample.reshape(DEC_BATCH * DEC_SEQ, D_MODEL)], axis=0)
    cvec = jnp.concatenate([c_ctx[None, :], c, jnp.zeros((MOD_ROWS - 1 - DEC_BATCH, D_MODEL), _f32)], axis=0)
    mod = _modulation(cvec, w_mod, b_mod).reshape(DEPTH, MOD_ROWS, 1, N_MOD * D_MODEL)
    bias = _na_bias_tables(na_rpb)
    cos_t, sin_t = _rope_tables()
    gdim = A_WIDTH // A_GROUPS
    sgu_b_lanes = jnp.broadcast_to(sgu_b[:, :, :, None], (DEPTH, A_GROUPS, CHUNK, gdim))

    vec = lambda a: a[:, None, :]
    g_pre_mix, g_post_mix, g_pre_ffn, g_post_ffn = vec(g_pre_mix), vec(g_post_mix), vec(g_pre_ffn), vec(g_post_ffn)
    sgu_ln_g, sgu_ln_b = vec(sgu_ln_g), vec(sgu_ln_b)
    sgu_w = sgu_w.astype(_bf16)

    states = ()
    for l in range(DEPTH):
        if l == 0:
            w_in_l = w_in
            ctx_jobs = [_CastJob(w_ff1, 0, CTX_STEPS, split_rows=False), _CastJob(w_ff2, 0, CTX_STEPS, split_rows=True)]
            na_jobs = [_CastJob(w, 0, NA_STEPS, split_rows=True) for w in (w_pa, w_pb, w_out)]
        else:
            w_in_l, ctx_jobs, na_jobs = w_in_b[None], [], []
        proj = _in_proj(x, mod, g_pre_mix, w_in_l, 0, cos_t, sin_t, l)
        ya = _sgu(proj, sgu_ln_g, sgu_ln_b, sgu_w, sgu_b_lanes, l)
        yb, k_state, v_state, *ctx_cast = _ctx_attention(proj, l, states, ctx_jobs)
        states = (k_state, v_state)
        yb, *na_cast = _na_attention(proj, cache_ctx_k, cache_ctx_v, bias, l, yb, na_jobs)
        if l == 0:
            (w_ff1_b, w_ff2_b), (w_pa_b, w_pb_b, w_out_b) = ctx_cast, na_cast
        x = _mix(proj, ya, yb, w_pa_b, w_pb_b, w_out_b, x, mod, g_post_mix, l)
        if l + 1 < DEPTH:
            nxt = l + 1
            jobs = [_CastJob(w, nxt, FFN_CAST_STEPS, split_rows=True) for w in (w_in, w_ff1, w_ff2, w_out)]
            jobs += [_CastJob(w, nxt, FFN_CAST_STEPS // 2, split_rows=True, steps_per_slab=2) for w in (w_pa, w_pb)]
            x, w_in_b, w_ff1_b, w_ff2_b, w_out_b, w_pa_b, w_pb_b = _ffn(
                x, mod, g_pre_ffn, g_post_ffn, w_ff1_b, w_ff2_b, l, cast_jobs=jobs)
        else:
            n_ctx_tiles = N_CTX_TOK // FFN_TM
            last = functools.partial(_ffn, x, mod, g_pre_ffn, g_post_ffn, w_ff1_b, w_ff2_b, l)
            y_prompt, = last(tile0=0, n_tiles=n_ctx_tiles)
            y_sample, = last(tile0=n_ctx_tiles, n_tiles=N_TOK // FFN_TM - n_ctx_tiles)
    return (y_prompt.reshape(BATCH, SEQ, D_MODEL), y_sample.reshape(DEC_BATCH, DEC_SEQ, D_MODEL), states[0], states[1])
```

```python
import functools

import numpy as np
import jax
import jax.numpy as jnp
from jax import lax
from jax.experimental import pallas as pl
from jax.experimental.pallas import tpu as pltpu

D_MODEL = 2048
BATCH = 32
SEQ = 256
DEPTH = 4
DEC_BATCH = 2
DEC_SEQ = 1024
PAST_LEN = 512
GRID_W = 64
CHUNK = 128
A_WIDTH = 1024
A_GROUPS = 8
NA_HEADS = 16
NA_HEAD_DIM = 64
NA_WIDTH = NA_HEADS * NA_HEAD_DIM
NA_KH_MAX = 8
NA_KW = 16
D_FF = 4 * D_MODEL
ROPE_THETA = 10000.0
EPS = 1e-6
N_MOD = 6
IN_COLS = 2 * A_WIDTH + 3 * NA_WIDTH + 2 * D_MODEL

N_CTX_TOK = BATCH * SEQ
N_TOK = N_CTX_TOK + DEC_BATCH * DEC_SEQ
ROWS = DEC_SEQ // GRID_W
NA_KH = min(NA_KH_MAX, ROWS)
MOD_ROWS = 8
LANES = 128
HEADS_PER_BLOCK = LANES // NA_HEAD_DIM
N_HEAD_BLOCKS = NA_HEADS // HEADS_PER_BLOCK

NA_QROWS = 2
NA_QBLK = NA_QROWS * GRID_W
NA_NPAIR = ROWS // NA_QROWS
NA_TOEP_ROWS = 2 * NA_KH_MAX - 1

VMEM_LIMIT = 56 * 1024 * 1024

_f32 = jnp.float32
_bf16 = jnp.bfloat16
_NT = (((1,), (1,)), ((), ()))


def _na_window_plan():
    plan = []
    for p in range(NA_NPAIR):
        rows = [NA_QROWS * p + rr for rr in range(NA_QROWS)]
        starts = [int(np.clip(r - NA_KH // 2, 0, ROWS - NA_KH)) for r in rows]
        w0 = min(starts) // 2 * 2
        w1 = -(-(max(starts) + NA_KH) // 2) * 2
        per_row = []
        for r, rs in zip(rows, starts):
            blocks = []
            for i in range(w0, w1, 2):
                keep_lo = rs <= i < rs + NA_KH
                keep_hi = rs <= i + 1 < rs + NA_KH
                a = i - r + NA_KH_MAX - 1
                assert not (keep_lo or keep_hi) or 0 <= a <= NA_TOEP_ROWS - 2
                blocks.append((a, keep_lo, keep_hi))
            per_row.append(blocks)
        plan.append((w0, w1 - w0, per_row))
    return plan


NA_PLAN = _na_window_plan()


def _mod_row(tile, tm):
    return jnp.maximum((tile * tm - N_CTX_TOK) // DEC_SEQ + 1, 0)


def _sigmoid(x):
    return 1.0 / (1.0 + jnp.exp(-x))


def _gelu(x):
    return 0.5 * x * (1.0 + jnp.tanh(np.sqrt(2.0 / np.pi).astype(np.float32) * (x + 0.044715 * (x * x * x))))


SUMSQ_ROWS = 256
NORM_ROWS = 64
N_LANE_BLOCKS = D_MODEL // LANES


def _row_sumsq_pass(src_ref, ss_ref, n_rows):
    def body(c, carry):
        rows = pl.ds(pl.multiple_of(c * SUMSQ_ROWS, SUMSQ_ROWS), SUMSQ_ROWS)
        acc = jnp.zeros((SUMSQ_ROWS, LANES), _f32)
        for cb in range(N_LANE_BLOCKS):
            v = src_ref[rows, cb * LANES:(cb + 1) * LANES]
            acc = acc + v * v
        ss_ref[rows, :] = jnp.broadcast_to(jnp.sum(acc, axis=-1, keepdims=True), (SUMSQ_ROWS, LANES))
        return carry
    lax.fori_loop(0, n_rows // SUMSQ_ROWS, body, 0)


def _row_rescale_pass(ss_ref, n_rows, block_fn):
    def body(c, carry):
        rows = pl.ds(pl.multiple_of(c * NORM_ROWS, NORM_ROWS), NORM_ROWS)
        r = lax.rsqrt(ss_ref[rows, :] * (1.0 / D_MODEL) + EPS)
        for cb in range(N_LANE_BLOCKS):
            block_fn(rows, slice(cb * LANES, (cb + 1) * LANES), r)
        return carry
    lax.fori_loop(0, n_rows // NORM_ROWS, body, 0)


def _modnorm_to(x_ref, h_ref, ss_ref, n_rows, g_ref, scale_ref, shift_ref, also=None):
    _row_sumsq_pass(x_ref, ss_ref, n_rows)

    def block(rows, cols, r):
        y = x_ref[rows, cols] * r
        h = (y * g_ref[0, :, cols]) * (1.0 + scale_ref[0, 0, :, cols]) + shift_ref[0, 0, :, cols]
        h_ref[rows, cols] = h.astype(_bf16)
        if also is not None:
            also(rows, cols)
    _row_rescale_pass(ss_ref, n_rows, block)


def _residual_rmsnorm_inplace(o_ref, x_ref, ss_ref, n_rows, gate_ref, g_ref):
    _row_sumsq_pass(o_ref, ss_ref, n_rows)

    def block(rows, cols, r):
        y = (o_ref[rows, cols] * r) * g_ref[0, :, cols]
        o_ref[rows, cols] = x_ref[rows, cols] + gate_ref[0, 0, :, cols] * y
    _row_rescale_pass(ss_ref, n_rows, block)


def _mod_kernel(c_ref, w_ref, b_ref, o_ref):
    c = c_ref[...]
    s = (c * _sigmoid(c)).astype(_bf16)
    o_ref[0] = jnp.dot(s, w_ref[0].astype(_bf16), preferred_element_type=_f32) + b_ref[0]


def _modulation(cvec, w_mod, b_mod):
    tn = 1024
    n_cols = N_MOD * D_MODEL
    return pl.pallas_call(
        _mod_kernel,
        out_shape=jax.ShapeDtypeStruct((DEPTH, MOD_ROWS, n_cols), _f32),
        grid=(DEPTH, n_cols // tn),
        in_specs=[
            pl.BlockSpec((MOD_ROWS, D_MODEL), lambda l, j: (0, 0)),
            pl.BlockSpec((1, D_MODEL, tn), lambda l, j: (l, 0, j)),
            pl.BlockSpec((1, 1, tn), lambda l, j: (l, 0, j)),
        ],
        out_specs=pl.BlockSpec((1, MOD_ROWS, tn), lambda l, j: (l, 0, j)),
        compiler_params=pltpu.CompilerParams(
            dimension_semantics=("arbitrary", "arbitrary"), vmem_limit_bytes=VMEM_LIMIT),
        name="modulation",
    )(cvec, w_mod, b_mod.reshape(DEPTH, 1, n_cols))


def _toeplitz_kernel(rpb_ref, onehot_ref, mask_ref, o_ref):
    o_ref[...] = jnp.dot(rpb_ref[...], onehot_ref[...], preferred_element_type=_f32,
                         precision=lax.Precision.HIGHEST) + mask_ref[...]


def _na_bias_tables(na_rpb):
    n_pair = NA_TOEP_ROWS - 1
    n_col = 2 * NA_KW - 1
    n_col_pad = 32
    cols = np.arange(GRID_W)
    cstart = np.clip(cols - NA_KW // 2, 0, GRID_W - NA_KW)
    in_win = (cols[None, :] >= cstart[:, None]) & (cols[None, :] < cstart[:, None] + NA_KW)
    col_idx = np.clip(cols[None, :] - cols[:, None] + NA_KW - 1, 0, n_col - 1)
    onehot = np.zeros((2, n_col_pad, GRID_W, 2, GRID_W), np.float32)
    for half in range(2):
        onehot[half, :, :, half, :] = (col_idx[None] == np.arange(n_col_pad)[:, None, None]) & in_win[None]
    onehot = onehot.reshape(2 * n_col_pad, 2 * GRID_W * GRID_W)
    mask = np.where(np.broadcast_to(in_win[:, None, :], (GRID_W, 2, GRID_W)), 0.0, -np.inf)
    mask = mask.reshape(1, -1).astype(np.float32)
    rpb_pad = jnp.pad(na_rpb, ((0, 0), (0, 0), (0, 0), (0, n_col_pad - n_col)))
    rpb_pairs = jnp.concatenate([rpb_pad[:, :, :-1], rpb_pad[:, :, 1:]], axis=-1)
    n_rows = DEPTH * NA_HEADS * n_pair
    row_block = 128
    pairs = pl.pallas_call(
        _toeplitz_kernel,
        out_shape=jax.ShapeDtypeStruct((n_rows, onehot.shape[1]), _f32),
        grid=(n_rows // row_block,),
        in_specs=[
            pl.BlockSpec((row_block, 2 * n_col_pad), lambda r: (r, 0)),
            pl.BlockSpec(onehot.shape, lambda r: (0, 0)),
            pl.BlockSpec(mask.shape, lambda r: (0, 0)),
        ],
        out_specs=pl.BlockSpec((row_block, onehot.shape[1]), lambda r: (r, 0)),
        compiler_params=pltpu.CompilerParams(dimension_semantics=("arbitrary",), vmem_limit_bytes=VMEM_LIMIT),
        name="rpb_toeplitz",
    )(rpb_pairs.reshape(n_rows, 2 * n_col_pad), jnp.asarray(onehot), jnp.asarray(mask))
    return pairs.reshape(DEPTH, N_HEAD_BLOCKS, HEADS_PER_BLOCK, n_pair, GRID_W, 2 * GRID_W)


def _rope_tables():
    t = jnp.arange(DEC_SEQ)
    half = NA_HEAD_DIM // 4
    freqs = ROPE_THETA ** (-jnp.arange(half, dtype=_f32) / half)
    ang_r = (t // GRID_W).astype(_f32)[:, None] * freqs[None, :]
    ang_c = (t % GRID_W).astype(_f32)[:, None] * freqs[None, :]
    cos_h = jnp.concatenate([jnp.cos(ang_r)] * 2 + [jnp.cos(ang_c)] * 2, axis=-1)
    sin_h = jnp.concatenate([-jnp.sin(ang_r), jnp.sin(ang_r), -jnp.sin(ang_c), jnp.sin(ang_c)], axis=-1)
    return jnp.tile(cos_h, (1, HEADS_PER_BLOCK)), jnp.tile(sin_h, (1, HEADS_PER_BLOCK))


IN_TM = 1024
IN_TN_F32_WEIGHT = 1024
IN_TN_BF16_WEIGHT = 1536
IN_PROJ_COLS = 2 * A_WIDTH + 3 * NA_WIDTH + 2 * D_MODEL
ROPE_COLS = (2 * A_WIDTH, 2 * A_WIDTH + 2 * NA_WIDTH)


def _in_kernel(tn, tile0, x_ref, shift_ref, scale_ref, g_ref, w_ref, cos_ref, sin_ref, *rest):
    o_ref, h_ref, ss_ref = rest[-3:]
    i = tile0 + pl.program_id(0)
    j = pl.program_id(1)

    @pl.when(j == 0)
    def _():
        _modnorm_to(x_ref, h_ref, ss_ref, IN_TM, g_ref, scale_ref, shift_ref)

    o_ref[...] = jnp.dot(h_ref[...], w_ref[0].astype(_bf16), preferred_element_type=_f32)

    for jt in range(IN_PROJ_COLS // tn):
        blocks = [cb for cb in range(tn // LANES) if ROPE_COLS[0] <= jt * tn + cb * LANES < ROPE_COLS[1]]
        if not blocks:
            continue

        @pl.when(jnp.logical_and(i >= N_CTX_TOK // IN_TM, j == jt))
        def _(blocks=blocks):
            lane = lax.broadcasted_iota(jnp.int32, (IN_TM, LANES), 1)
            first_half = (lane % (NA_HEAD_DIM // 2)) < (NA_HEAD_DIM // 4)
            cos = cos_ref[...]
            sin = sin_ref[...]
            for cb in blocks:
                cols = slice(cb * LANES, (cb + 1) * LANES)
                xb = o_ref[:, cols]
                partner = jnp.where(first_half, pltpu.roll(xb, LANES - NA_HEAD_DIM // 4, axis=1),
                                    pltpu.roll(xb, NA_HEAD_DIM // 4, axis=1))
                o_ref[:, cols] = xb * cos + partner * sin


def _mod_spec(layer, tm, chunk, tile0=0):
    return pl.BlockSpec((1, 1, 1, D_MODEL), lambda i, j: (layer, _mod_row(tile0 + i, tm), 0, chunk))


def _layer_vec_spec(layer, width):
    return pl.BlockSpec((1, 1, width), lambda *_: (layer, 0, 0))


def _in_proj(x, mod, g, w_in, w_layer, cos_t, sin_t, layer, tile0=0, proj=None):
    n_m = x.shape[0] // IN_TM
    tn = IN_TN_BF16_WEIGHT if w_in.dtype == _bf16 else IN_TN_F32_WEIGHT
    carried = () if proj is None else (proj,)
    return pl.pallas_call(
        functools.partial(_in_kernel, tn, tile0),
        out_shape=jax.ShapeDtypeStruct((N_TOK, IN_PROJ_COLS), _f32),
        grid=(n_m, IN_PROJ_COLS // tn),
        in_specs=[
            pl.BlockSpec((IN_TM, D_MODEL), lambda i, j: (i, 0)),
            _mod_spec(layer, IN_TM, 0, tile0), _mod_spec(layer, IN_TM, 1, tile0),
            _layer_vec_spec(layer, D_MODEL),
            pl.BlockSpec((1, D_MODEL, tn), lambda i, j: (w_layer, 0, j)),
            pl.BlockSpec((DEC_SEQ, LANES), lambda i, j: (0, 0)),
            pl.BlockSpec((DEC_SEQ, LANES), lambda i, j: (0, 0)),
        ] + [pl.BlockSpec(memory_space=pl.ANY)] * len(carried),
        out_specs=pl.BlockSpec((IN_TM, tn), lambda i, j: (tile0 + i, j)),
        scratch_shapes=[pltpu.VMEM((IN_TM, D_MODEL), _bf16), pltpu.VMEM((IN_TM, LANES), _f32)],
        input_output_aliases={7: 0} if carried else {},
        compiler_params=pltpu.CompilerParams(
            dimension_semantics=("arbitrary", "arbitrary"), vmem_limit_bytes=VMEM_LIMIT),
        name="in_proj",
    )(x, mod, mod, g, w_in, cos_t, sin_t, *carried)


SGU_TM = 1024


def _sgu_kernel(au_ref, av_ref, lng_ref, lnb_ref, ws_ref, bs_ref, o_ref):
    gdim = A_WIDTH // A_GROUPS

    def body(c, carry):
        rows = pl.ds(pl.multiple_of(c * CHUNK, CHUNK), CHUNK)
        u = _gelu(au_ref[rows, :])
        gv = _gelu(av_ref[rows, :])
        mu = jnp.mean(gv, axis=-1, keepdims=True)
        var = jnp.mean(jnp.square(gv - mu), axis=-1, keepdims=True)
        vn = ((gv - mu) * lax.rsqrt(var + EPS) * lng_ref[0] + lnb_ref[0]).astype(_bf16)
        for g in range(A_GROUPS):
            cols = slice(g * gdim, (g + 1) * gdim)
            s = jnp.dot(ws_ref[0, g], vn[:, cols], preferred_element_type=_f32) + bs_ref[0, g]
            o_ref[rows, cols] = (u[:, cols] * s).astype(_bf16)
        return carry

    lax.fori_loop(0, SGU_TM // CHUNK, body, 0)


def _sgu(proj, ln_g, ln_b, w_s, b_s, layer):
    gdim = A_WIDTH // A_GROUPS
    return pl.pallas_call(
        _sgu_kernel,
        out_shape=jax.ShapeDtypeStruct((N_TOK, A_WIDTH), _bf16),
        grid=(N_TOK // SGU_TM,),
        in_specs=[
            pl.BlockSpec((SGU_TM, A_WIDTH), lambda i: (i, 0)),
            pl.BlockSpec((SGU_TM, A_WIDTH), lambda i: (i, 1)),
            _layer_vec_spec(layer, A_WIDTH),
            _layer_vec_spec(layer, A_WIDTH),
            pl.BlockSpec((1, A_GROUPS, CHUNK, CHUNK), lambda i: (layer, 0, 0, 0)),
            pl.BlockSpec((1, A_GROUPS, CHUNK, gdim), lambda i: (layer, 0, 0, 0)),
        ],
        out_specs=pl.BlockSpec((SGU_TM, A_WIDTH), lambda i: (i, 0)),
        compiler_params=pltpu.CompilerParams(dimension_semantics=("arbitrary",), vmem_limit_bytes=VMEM_LIMIT),
        name="sgu",
    )(proj, proj, ln_g, ln_b, w_s, b_s)


ATT_SCALE = NA_HEAD_DIM ** -0.5
Q_LANE_BLOCK = (2 * A_WIDTH) // LANES
K_LANE_BLOCK = (2 * A_WIDTH + NA_WIDTH) // LANES
V_LANE_BLOCK = (2 * A_WIDTH + 2 * NA_WIDTH) // LANES


def _softmax_pv(s, v):
    e = jnp.exp(s - jnp.max(s, axis=-1, keepdims=True))
    return jnp.dot(e.astype(_bf16), v, preferred_element_type=_f32) / jnp.sum(e, axis=-1, keepdims=True)


CTX_SEQS_PER_STEP = 4
CTX_TM = CTX_SEQS_PER_STEP * SEQ


def _head_lane_masks(rows):
    lane = lax.broadcasted_iota(jnp.int32, (rows, LANES), 1)
    return [jnp.logical_and(lane >= hh * NA_HEAD_DIM, lane < (hh + 1) * NA_HEAD_DIM)
            for hh in range(HEADS_PER_BLOCK)]


def _select_heads(masks, per_head):
    out = per_head[-1]
    for mask, val in zip(masks[:-1], per_head[:-1]):
        out = jnp.where(mask, val, out)
    return out


class _CastJob:
    def __init__(self, w, layer, n_slabs, split_rows, steps_per_slab=1):
        _, r, c = w.shape
        self.w, self.layer = w, layer
        self.out_shape = jax.ShapeDtypeStruct((r, c), _bf16)
        self.block = (r // n_slabs, c) if split_rows else (r, c // n_slabs)
        self.split_rows = split_rows
        self.steps_per_slab = steps_per_slab

    def specs(self, step_of):
        slab_of = lambda *g: step_of(*g) // self.steps_per_slab
        pos = (lambda *g: (slab_of(*g), 0)) if self.split_rows else (lambda *g: (0, slab_of(*g)))
        layer = self.layer
        return (pl.BlockSpec((1,) + self.block, lambda *g: (layer,) + pos(*g)), pl.BlockSpec(self.block, pos))


def _run_cast_jobs(src_refs, dst_refs):
    for src, dst in zip(src_refs, dst_refs):
        dst[...] = src[0].astype(_bf16)


def _ctx_attn_kernel(n_cast, q_ref, k_ref, v_ref, *rest):
    outs = rest[len(rest) - 3 - n_cast:]
    o_ref, kst_ref, vst_ref = outs[:3]
    _run_cast_jobs(rest[:n_cast], outs[3:])
    masks = _head_lane_masks(SEQ)
    seq_outs = []
    for s in range(CTX_SEQS_PER_STEP):
        rows = slice(s * SEQ, (s + 1) * SEQ)
        q32 = q_ref[rows, :] * ATT_SCALE
        k32 = k_ref[rows, :]
        v32 = v_ref[rows, :]
        for hh in range(HEADS_PER_BLOCK):
            cols = slice(hh * NA_HEAD_DIM, (hh + 1) * NA_HEAD_DIM)
            kst_ref[s, 0, hh] = k32[:, cols]
            vst_ref[s, 0, hh] = v32[:, cols]
        kb = k32.astype(_bf16)
        vb = v32.astype(_bf16)
        per_head = []
        for mask in masks:
            qh = jnp.where(mask, q32, 0.0).astype(_bf16)
            sc = lax.dot_general(qh, kb, _NT, preferred_element_type=_f32)
            per_head.append(_softmax_pv(sc, vb))
        seq_outs.append(_select_heads(masks, per_head))
    o_ref[...] = jnp.concatenate(seq_outs, axis=0).astype(_bf16)


CTX_GRID = (N_CTX_TOK // CTX_TM, N_HEAD_BLOCKS)
CTX_STEPS = CTX_GRID[0] * CTX_GRID[1]


def _ctx_attention(proj, layer, states, cast_jobs):
    state_shape = jax.ShapeDtypeStruct((BATCH, DEPTH, NA_HEADS, SEQ, NA_HEAD_DIM), _f32)
    state_spec = pl.BlockSpec((CTX_SEQS_PER_STEP, 1, HEADS_PER_BLOCK, SEQ, NA_HEAD_DIM),
                              lambda i, h: (i, layer, h, 0, 0))
    cast_specs = [job.specs(lambda i, h: i * CTX_GRID[1] + h) for job in cast_jobs]
    n_cast, n_state_in = len(cast_jobs), len(states)
    return pl.pallas_call(
        functools.partial(_ctx_attn_kernel, n_cast),
        out_shape=(jax.ShapeDtypeStruct((N_TOK, NA_WIDTH), _bf16), state_shape, state_shape)
        + tuple(job.out_shape for job in cast_jobs),
        grid=CTX_GRID,
        in_specs=[
            pl.BlockSpec((CTX_TM, LANES), lambda i, h: (i, Q_LANE_BLOCK + h)),
            pl.BlockSpec((CTX_TM, LANES), lambda i, h: (i, K_LANE_BLOCK + h)),
            pl.BlockSpec((CTX_TM, LANES), lambda i, h: (i, V_LANE_BLOCK + h)),
        ] + [s[0] for s in cast_specs] + [pl.BlockSpec(memory_space=pl.ANY)] * n_state_in,
        out_specs=(pl.BlockSpec((CTX_TM, LANES), lambda i, h: (i, h)), state_spec, state_spec)
        + tuple(s[1] for s in cast_specs),
        input_output_aliases={3 + n_cast + n: 1 + n for n in range(n_state_in)},
        compiler_params=pltpu.CompilerParams(
            dimension_semantics=("arbitrary", "arbitrary"), vmem_limit_bytes=VMEM_LIMIT),
        name="ctx_attention",
    )(proj, proj, proj, *[job.w for job in cast_jobs], *states)


def _na_pair_bias(bias_ref, hh, blocks, lane_lo):
    pieces = []
    for a, keep_lo, keep_hi in blocks:
        if not (keep_lo or keep_hi):
            pieces.append(jnp.full((GRID_W, 2 * GRID_W), -jnp.inf, _f32))
            continue
        piece = bias_ref[0, 0, hh, a]
        if not keep_lo:
            piece = jnp.where(lane_lo, -jnp.inf, piece)
        if not keep_hi:
            piece = jnp.where(lane_lo, piece, -jnp.inf)
        pieces.append(piece)
    return jnp.concatenate(pieces, axis=-1)


def _na_attn_kernel(n_cast, q_ref, k_ref, v_ref, ck_ref, cv_ref, bias_ref, yb_hbm_ref, *rest):
    del yb_hbm_ref
    o_ref, e_refs = rest[n_cast], rest[-1]
    _run_cast_jobs(rest[:n_cast], rest[n_cast + 1:-1])
    lane_lo = lax.broadcasted_iota(jnp.int32, (GRID_W, 2 * GRID_W), 1) < GRID_W
    masks = _head_lane_masks(DEC_SEQ)
    q32 = q_ref[...] * ATT_SCALE
    ck = jnp.concatenate([ck_ref[0, 0, hh] for hh in range(HEADS_PER_BLOCK)], axis=-1)
    cv = jnp.concatenate([cv_ref[0, 0, hh] for hh in range(HEADS_PER_BLOCK)], axis=-1)
    keys = jnp.concatenate([k_ref[...], ck], axis=0).astype(_bf16)
    vals = jnp.concatenate([v_ref[...], cv], axis=0).astype(_bf16)
    ctx_cols = slice(DEC_SEQ, DEC_SEQ + PAST_LEN)
    head_outs = []
    for hh, mask in enumerate(masks):
        e_ref = e_refs.at[hh]
        e_ref[:, :DEC_SEQ] = jnp.zeros((DEC_SEQ, DEC_SEQ), _bf16)
        q = jnp.where(mask, q32, 0.0).astype(_bf16)
        s_all = lax.dot_general(q, keys, _NT, preferred_element_type=_f32)
        denoms = []
        for p, (w0, n_rows, per_row) in enumerate(NA_PLAN):
            qrows = slice(p * NA_QBLK, (p + 1) * NA_QBLK)
            kcols = slice(w0 * GRID_W, (w0 + n_rows) * GRID_W)
            bias = jnp.concatenate([_na_pair_bias(bias_ref, hh, blocks, lane_lo) for blocks in per_row], axis=0)
            s_win = s_all[qrows, kcols] + bias
            s_ctx = s_all[qrows, ctx_cols]
            m = jnp.maximum(jnp.max(s_win, axis=-1, keepdims=True), jnp.max(s_ctx, axis=-1, keepdims=True))
            e_win = jnp.exp(s_win - m)
            e_ctx = jnp.exp(s_ctx - m)
            denoms.append(jnp.sum(e_win, axis=-1, keepdims=True) + jnp.sum(e_ctx, axis=-1, keepdims=True))
            e_ref[qrows, kcols] = e_win.astype(_bf16)
            e_ref[qrows, ctx_cols] = e_ctx.astype(_bf16)
        half = DEC_SEQ // 2
        o = jnp.concatenate([jnp.dot(e_ref[r0:r0 + half, :], vals, preferred_element_type=_f32)
                             for r0 in (0, half)], axis=0)
        head_outs.append(o / jnp.concatenate(denoms, axis=0))
    o_ref[...] = _select_heads(masks, head_outs).astype(_bf16)


NA_GRID = (N_HEAD_BLOCKS, DEC_BATCH)
NA_STEPS = NA_GRID[0] * NA_GRID[1]


def _na_attention(proj, cache_k, cache_v, bias_tab, layer, yb, cast_jobs):
    tile0 = N_CTX_TOK // DEC_SEQ
    qkv_spec = lambda lane_block: pl.BlockSpec((DEC_SEQ, LANES), lambda h, b: (tile0 + b, lane_block + h))
    cache_spec = pl.BlockSpec((1, 1, HEADS_PER_BLOCK, PAST_LEN, NA_HEAD_DIM), lambda h, b: (b, layer, h, 0, 0))
    cast_specs = [job.specs(lambda h, b: h * NA_GRID[1] + b) for job in cast_jobs]
    return pl.pallas_call(
        functools.partial(_na_attn_kernel, len(cast_jobs)),
        out_shape=(jax.ShapeDtypeStruct((N_TOK, NA_WIDTH), _bf16),) + tuple(job.out_shape for job in cast_jobs),
        grid=NA_GRID,
        in_specs=[
            qkv_spec(Q_LANE_BLOCK), qkv_spec(K_LANE_BLOCK), qkv_spec(V_LANE_BLOCK),
            cache_spec, cache_spec,
            pl.BlockSpec((1, 1, HEADS_PER_BLOCK, NA_TOEP_ROWS - 1, GRID_W, 2 * GRID_W),
                         lambda h, b: (layer, h, 0, 0, 0, 0)),
            pl.BlockSpec(memory_space=pl.ANY),
        ] + [s[0] for s in cast_specs],
        out_specs=(pl.BlockSpec((DEC_SEQ, LANES), lambda h, b: (tile0 + b, h)),) + tuple(s[1] for s in cast_specs),
        scratch_shapes=[pltpu.VMEM((HEADS_PER_BLOCK, DEC_SEQ, DEC_SEQ + PAST_LEN), _bf16)],
        input_output_aliases={6: 0},
        compiler_params=pltpu.CompilerParams(
            dimension_semantics=("arbitrary", "arbitrary"), vmem_limit_bytes=VMEM_LIMIT),
        name="na_attention",
    )(proj, proj, proj, cache_k, cache_v, bias_tab, yb, *[job.w for job in cast_jobs])


MIX_TM = 512
MIX_TC = 1024
GA_COL_TILE = (2 * A_WIDTH + 3 * NA_WIDTH) // MIX_TC
GB_COL_TILE = GA_COL_TILE + D_MODEL // MIX_TC


def _mix_kernel(ga_ref, gb_ref, ya_ref, yb_ref, wpa_ref, wpb_ref, wout_ref, x_ref, gate_ref, g_ref, *rest):
    o_ref, ss_ref = rest[-2:]
    j = pl.program_id(1)

    @pl.when(j == 0)
    def _():
        o_ref[...] = jnp.zeros_like(o_ref)

    pa = jnp.dot(ya_ref[...], wpa_ref[...], preferred_element_type=_f32)
    pb = jnp.dot(yb_ref[...], wpb_ref[...], preferred_element_type=_f32)
    merged = (_sigmoid(ga_ref[...]) * pa + _sigmoid(gb_ref[...]) * pb).astype(_bf16)
    o_ref[...] += jnp.dot(merged, wout_ref[...], preferred_element_type=_f32)

    @pl.when(j == pl.num_programs(1) - 1)
    def _():
        _residual_rmsnorm_inplace(o_ref, x_ref, ss_ref, MIX_TM, gate_ref, g_ref)


def _mix(proj, ya, yb, w_pa, w_pb, w_out, x, mod, g_post, layer, tile0=0, x_out=None):
    carried = () if x_out is None else (x_out,)
    return pl.pallas_call(
        _mix_kernel,
        out_shape=jax.ShapeDtypeStruct((N_TOK, D_MODEL), _f32),
        grid=(x.shape[0] // MIX_TM, D_MODEL // MIX_TC),
        in_specs=[
            pl.BlockSpec((MIX_TM, MIX_TC), lambda i, j: (tile0 + i, GA_COL_TILE + j)),
            pl.BlockSpec((MIX_TM, MIX_TC), lambda i, j: (tile0 + i, GB_COL_TILE + j)),
            pl.BlockSpec((MIX_TM, A_WIDTH), lambda i, j: (tile0 + i, 0)),
            pl.BlockSpec((MIX_TM, NA_WIDTH), lambda i, j: (tile0 + i, 0)),
            pl.BlockSpec((A_WIDTH, MIX_TC), lambda i, j: (0, j)),
            pl.BlockSpec((NA_WIDTH, MIX_TC), lambda i, j: (0, j)),
            pl.BlockSpec((MIX_TC, D_MODEL), lambda i, j: (j, 0)),
            pl.BlockSpec((MIX_TM, D_MODEL), lambda i, j: (i, 0)),
            _mod_spec(layer, MIX_TM, 2, tile0),
            _layer_vec_spec(layer, D_MODEL),
        ] + [pl.BlockSpec(memory_space=pl.ANY)] * len(carried),
        out_specs=pl.BlockSpec((MIX_TM, D_MODEL), lambda i, j: (tile0 + i, 0)),
        scratch_shapes=[pltpu.VMEM((MIX_TM, LANES), _f32)],
        input_output_aliases={10: 0} if carried else {},
        compiler_params=pltpu.CompilerParams(
            dimension_semantics=("arbitrary", "arbitrary"), vmem_limit_bytes=VMEM_LIMIT),
        name="mix_out",
    )(proj, proj, ya, yb, w_pa, w_pb, w_out, x, mod, g_post, *carried)


FFN_TM = 1024
FFN_TF = 512


FFN_OUT_SPLIT = 2


def _ffn_kernel(n_cast, x_ref, shift_ref, scale_ref, gate_ref, gpre_ref, gpost_ref, w1_ref, w2_ref, *rest):
    o_ref, h_ref, ss_ref = rest[n_cast], rest[-2], rest[-1]
    _run_cast_jobs(rest[:n_cast], rest[n_cast + 1:-2])
    k = pl.program_id(1)

    @pl.when(k == 0)
    def _():
        def zero_out(rows, cols):
            o_ref[rows, cols] = jnp.zeros((NORM_ROWS, LANES), _f32)
        _modnorm_to(x_ref, h_ref, ss_ref, FFN_TM, gpre_ref, scale_ref, shift_ref, also=zero_out)

    a = jnp.dot(h_ref[...], w1_ref[...], preferred_element_type=_f32)
    a = jnp.square(jnp.maximum(a, 0.0)).astype(_bf16)
    width = D_MODEL // FFN_OUT_SPLIT
    for s in range(FFN_OUT_SPLIT):
        cols = slice(s * width, (s + 1) * width)
        o_ref[:, cols] += jnp.dot(a, w2_ref[:, cols], preferred_element_type=_f32)

    @pl.when(k == pl.num_programs(1) - 1)
    def _():
        _residual_rmsnorm_inplace(o_ref, x_ref, ss_ref, FFN_TM, gate_ref, gpost_ref)


FFN_K_STEPS = D_FF // FFN_TF
FFN_CAST_TILES = 8
FFN_CAST_STEPS = FFN_CAST_TILES * FFN_K_STEPS


def _ffn_cast_step(i, k):
    return jnp.where(i < FFN_CAST_TILES, i * FFN_K_STEPS + k, FFN_CAST_STEPS - 1)


def _ffn(x, mod, g_pre, g_post, w1, w2, layer, tile0=0, n_tiles=N_TOK // FFN_TM, cast_jobs=()):
    assert not cast_jobs or n_tiles >= FFN_CAST_TILES
    cast_specs = [job.specs(_ffn_cast_step) for job in cast_jobs]
    outs = pl.pallas_call(
        functools.partial(_ffn_kernel, len(cast_jobs)),
        out_shape=(jax.ShapeDtypeStruct((n_tiles * FFN_TM, D_MODEL), _f32),)
        + tuple(job.out_shape for job in cast_jobs),
        grid=(n_tiles, FFN_K_STEPS),
        in_specs=[
            pl.BlockSpec((FFN_TM, D_MODEL), lambda i, k: (tile0 + i, 0)),
            _mod_spec(layer, FFN_TM, 3, tile0), _mod_spec(layer, FFN_TM, 4, tile0), _mod_spec(layer, FFN_TM, 5, tile0),
            _layer_vec_spec(layer, D_MODEL),
            _layer_vec_spec(layer, D_MODEL),
            pl.BlockSpec((D_MODEL, FFN_TF), lambda i, k: (0, k)),
            pl.BlockSpec((FFN_TF, D_MODEL), lambda i, k: (k, 0)),
        ] + [s[0] for s in cast_specs],
        out_specs=(pl.BlockSpec((FFN_TM, D_MODEL), lambda i, k: (i, 0)),) + tuple(s[1] for s in cast_specs),
        scratch_shapes=[pltpu.VMEM((FFN_TM, D_MODEL), _bf16), pltpu.VMEM((FFN_TM, LANES), _f32)],
        compiler_params=pltpu.CompilerParams(
            dimension_semantics=("arbitrary", "arbitrary"), vmem_limit_bytes=VMEM_LIMIT),
        name="ffn",
    )(x, mod, mod, mod, g_pre, g_post, w1, w2, *[job.w for job in cast_jobs])
    return outs


def kernel(x_prompt, x_sample, cache_ctx_k, cache_ctx_v, c, c_ctx, w_mod, b_mod, g_pre_mix, g_post_mix, g_pre_ffn,
           g_post_ffn, w_in, sgu_ln_g, sgu_ln_b, sgu_w, sgu_b, na_rpb, w_pa, w_pb, w_out, w_ff1, w_ff2):
    x_streams = (x_prompt.reshape(N_CTX_TOK, D_MODEL), x_sample.reshape(DEC_BATCH * DEC_SEQ, D_MODEL))
    cvec =jnp.concatenate([c_ctx[None, :], c, jnp.zeros((MOD_ROWS - 1 - DEC_BATCH, D_MODEL), _f32)], axis=0)
    mod = _modulation(cvec, w_mod, b_mod).reshape(DEPTH, MOD_ROWS, 1, N_MOD * D_MODEL)
    bias = _na_bias_tables(na_rpb)
    cos_t, sin_t = _rope_tables()
    gdim = A_WIDTH // A_GROUPS
    sgu_b_lanes = jnp.broadcast_to(sgu_b[:, :, :, None], (DEPTH, A_GROUPS, CHUNK, gdim))

    vec = lambda a: a[:, None, :]
    g_pre_mix, g_post_mix, g_pre_ffn, g_post_ffn = vec(g_pre_mix), vec(g_post_mix), vec(g_pre_ffn), vec(g_post_ffn)
    sgu_ln_g, sgu_ln_b = vec(sgu_ln_g), vec(sgu_ln_b)
    sgu_w = sgu_w.astype(_bf16)

    states = ()
    for l in range(DEPTH):
        if l == 0:
            w_in_l = w_in
            ctx_jobs = [_CastJob(w_ff1, 0, CTX_STEPS, split_rows=False), _CastJob(w_ff2, 0, CTX_STEPS, split_rows=True)]
            na_jobs = [_CastJob(w, 0, NA_STEPS, split_rows=True) for w in (w_pa, w_pb, w_out)]
        else:
            w_in_l, ctx_jobs, na_jobs = w_in_b[None], [], []
        streams = x_streams if l == 0 else (x,)
        proj, row0 = None, 0
        for xs in streams:
            proj = _in_proj(xs, mod, g_pre_mix, w_in_l, 0, cos_t, sin_t, l, tile0=row0 // IN_TM, proj=proj)
            row0 += xs.shape[0]
        ya = _sgu(proj, sgu_ln_g, sgu_ln_b, sgu_w, sgu_b_lanes, l)
        yb, k_state, v_state, *ctx_cast = _ctx_attention(proj, l, states, ctx_jobs)
        states = (k_state, v_state)
        yb, *na_cast = _na_attention(proj, cache_ctx_k, cache_ctx_v, bias, l, yb, na_jobs)
        if l == 0:
            (w_ff1_b, w_ff2_b), (w_pa_b, w_pb_b, w_out_b) = ctx_cast, na_cast
        x, row0 = None, 0
        for xs in streams:
            x = _mix(proj, ya, yb, w_pa_b, w_pb_b, w_out_b, xs, mod, g_post_mix, l, tile0=row0 // MIX_TM, x_out=x)
            row0 += xs.shape[0]
        if l + 1 < DEPTH:
            nxt = l + 1
            jobs = [_CastJob(w, nxt, FFN_CAST_STEPS, split_rows=True) for w in (w_in, w_ff1, w_ff2, w_out)]
            jobs += [_CastJob(w, nxt, FFN_CAST_STEPS // 2, split_rows=True, steps_per_slab=2) for w in (w_pa, w_pb)]
            x, w_in_b, w_ff1_b, w_ff2_b, w_out_b, w_pa_b, w_pb_b = _ffn(
                x, mod, g_pre_ffn, g_post_ffn, w_ff1_b, w_ff2_b, l, cast_jobs=jobs)
        else:
            n_ctx_tiles = N_CTX_TOK // FFN_TM
            last = functools.partial(_ffn, x, mod, g_pre_ffn, g_post_ffn, w_ff1_b, w_ff2_b, l)
            y_prompt, = last(tile0=0, n_tiles=n_ctx_tiles)
            y_sample, = last(tile0=n_ctx_tiles, n_tiles=N_TOK // FFN_TM - n_ctx_tiles)
    return (y_prompt.reshape(BATCH, SEQ, D_MODEL), y_sample.reshape(DEC_BATCH, DEC_SEQ, D_MODEL), states[0], states[1])
```

```python
import functools

import numpy as np
import jax
import jax.numpy as jnp
from jax import lax
from jax.experimental import pallas as pl
from jax.experimental.pallas import tpu as pltpu

D_MODEL = 2048
BATCH = 32
SEQ = 256
DEPTH = 4
DEC_BATCH = 2
DEC_SEQ = 1024
PAST_LEN = 512
GRID_W = 64
CHUNK = 128
A_WIDTH = 1024
A_GROUPS = 8
NA_HEADS = 16
NA_HEAD_DIM = 64
NA_WIDTH = NA_HEADS * NA_HEAD_DIM
NA_KH_MAX = 8
NA_KW = 16
D_FF = 4 * D_MODEL
ROPE_THETA = 10000.0
EPS = 1e-6
N_MOD = 6
IN_COLS = 2 * A_WIDTH + 3 * NA_WIDTH + 2 * D_MODEL

N_CTX_TOK = BATCH * SEQ
N_TOK = N_CTX_TOK + DEC_BATCH * DEC_SEQ
ROWS = DEC_SEQ // GRID_W
NA_KH = min(NA_KH_MAX, ROWS)
MOD_ROWS = 8
LANES = 128
HEADS_PER_BLOCK = LANES // NA_HEAD_DIM
N_HEAD_BLOCKS = NA_HEADS // HEADS_PER_BLOCK

NA_QROWS = 2
NA_QBLK = NA_QROWS * GRID_W
NA_NPAIR = ROWS // NA_QROWS
NA_TOEP_ROWS = 2 * NA_KH_MAX - 1

VMEM_LIMIT = 56 * 1024 * 1024

_f32 = jnp.float32
_bf16 = jnp.bfloat16
_NT = (((1,), (1,)), ((), ()))


def _na_window_plan():
    plan = []
    for p in range(NA_NPAIR):
        rows = [NA_QROWS * p + rr for rr in range(NA_QROWS)]
        starts = [int(np.clip(r - NA_KH // 2, 0, ROWS - NA_KH)) for r in rows]
        w0 = min(starts) // 2 * 2
        w1 = -(-(max(starts) + NA_KH) // 2) * 2
        per_row = []
        for r, rs in zip(rows, starts):
            blocks = []
            for i in range(w0, w1, 2):
                keep_lo = rs <= i < rs + NA_KH
                keep_hi = rs <= i + 1 < rs + NA_KH
                a = i - r + NA_KH_MAX - 1
                assert not (keep_lo or keep_hi) or 0 <= a <= NA_TOEP_ROWS - 2
                blocks.append((a, keep_lo, keep_hi))
            per_row.append(blocks)
        plan.append((w0, w1 - w0, per_row))
    return plan


NA_PLAN = _na_window_plan()


def _mod_row(tile, tm):
    return jnp.maximum((tile * tm - N_CTX_TOK) // DEC_SEQ + 1, 0)


def _sigmoid(x):
    return 1.0 / (1.0 + jnp.exp(-x))


def _gelu(x):
    return 0.5 * x * (1.0 + jnp.tanh(np.sqrt(2.0 / np.pi).astype(np.float32) * (x + 0.044715 * (x * x * x))))


SUMSQ_ROWS = 256
NORM_ROWS = 64
N_LANE_BLOCKS = D_MODEL // LANES


def _row_sumsq_pass(src_ref, ss_ref, n_rows):
    def body(c, carry):
        rows = pl.ds(pl.multiple_of(c * SUMSQ_ROWS, SUMSQ_ROWS), SUMSQ_ROWS)
        acc = jnp.zeros((SUMSQ_ROWS, LANES), _f32)
        for cb in range(N_LANE_BLOCKS):
            v = src_ref[rows, cb * LANES:(cb + 1) * LANES]
            acc = acc + v * v
        ss_ref[rows, :] = jnp.broadcast_to(jnp.sum(acc, axis=-1, keepdims=True), (SUMSQ_ROWS, LANES))
        return carry
    lax.fori_loop(0, n_rows // SUMSQ_ROWS, body, 0)


def _row_rescale_pass(ss_ref, n_rows, block_fn):
    def body(c, carry):
        rows = pl.ds(pl.multiple_of(c * NORM_ROWS, NORM_ROWS), NORM_ROWS)
        r = lax.rsqrt(ss_ref[rows, :] * (1.0 / D_MODEL) + EPS)
        for cb in range(N_LANE_BLOCKS):
            block_fn(rows, slice(cb * LANES, (cb + 1) * LANES), r)
        return carry
    lax.fori_loop(0, n_rows // NORM_ROWS, body, 0)


def _modnorm_to(x_ref, h_ref, ss_ref, n_rows, g_ref, scale_ref, shift_ref, also=None):
    _row_sumsq_pass(x_ref, ss_ref, n_rows)

    def block(rows, cols, r):
        y = x_ref[rows, cols] * r
        h = (y * g_ref[0, :, cols]) * (1.0 + scale_ref[0, 0, :, cols]) + shift_ref[0, 0, :, cols]
        h_ref[rows, cols] = h.astype(_bf16)
        if also is not None:
            also(rows, cols)
    _row_rescale_pass(ss_ref, n_rows, block)


def _residual_rmsnorm_inplace(o_ref, x_ref, ss_ref, n_rows, gate_ref, g_ref):
    _row_sumsq_pass(o_ref, ss_ref, n_rows)

    def block(rows, cols, r):
        y = (o_ref[rows, cols] * r) * g_ref[0, :, cols]
        o_ref[rows, cols] = x_ref[rows, cols] + gate_ref[0, 0, :, cols] * y
    _row_rescale_pass(ss_ref, n_rows, block)


def _mod_kernel(c_ref, w_ref, b_ref, o_ref):
    c = c_ref[...]
    s = (c * _sigmoid(c)).astype(_bf16)
    o_ref[0] = jnp.dot(s, w_ref[0].astype(_bf16), preferred_element_type=_f32) + b_ref[0]


def _modulation(cvec, w_mod, b_mod):
    tn = D_MODEL
    n_cols = N_MOD * D_MODEL
    return pl.pallas_call(
        _mod_kernel,
        out_shape=jax.ShapeDtypeStruct((DEPTH, MOD_ROWS, n_cols), _f32),
        grid=(DEPTH, n_cols // tn),
        in_specs=[
            pl.BlockSpec((MOD_ROWS, D_MODEL), lambda l, j: (0, 0)),
            pl.BlockSpec((1, D_MODEL, tn), lambda l, j: (l, 0, j)),
            pl.BlockSpec((1, 1, tn), lambda l, j: (l, 0, j)),
        ],
        out_specs=pl.BlockSpec((1, MOD_ROWS, tn), lambda l, j: (l, 0, j)),
        compiler_params=pltpu.CompilerParams(
            dimension_semantics=("arbitrary", "arbitrary"), vmem_limit_bytes=VMEM_LIMIT),
        name="modulation",
    )(cvec, w_mod, b_mod.reshape(DEPTH, 1, n_cols))


def _toeplitz_kernel(rpb_ref, onehot_ref, mask_ref, o_ref):
    o_ref[...] = jnp.dot(rpb_ref[...], onehot_ref[...], preferred_element_type=_f32,
                         precision=lax.Precision.HIGHEST) + mask_ref[...]


def _na_bias_tables(na_rpb):
    n_pair = NA_TOEP_ROWS - 1
    n_col = 2 * NA_KW - 1
    n_col_pad = 32
    cols = np.arange(GRID_W)
    cstart = np.clip(cols - NA_KW // 2, 0, GRID_W - NA_KW)
    in_win = (cols[None, :] >= cstart[:, None]) & (cols[None, :] < cstart[:, None] + NA_KW)
    col_idx = np.clip(cols[None, :] - cols[:, None] + NA_KW - 1, 0, n_col - 1)
    onehot = np.zeros((2, n_col_pad, GRID_W, 2, GRID_W), np.float32)
    for half in range(2):
        onehot[half, :, :, half, :] = (col_idx[None] == np.arange(n_col_pad)[:, None, None]) & in_win[None]
    onehot = onehot.reshape(2 * n_col_pad, 2 * GRID_W * GRID_W)
    mask = np.where(np.broadcast_to(in_win[:, None, :], (GRID_W, 2, GRID_W)), 0.0, -np.inf)
    mask = mask.reshape(1, -1).astype(np.float32)
    rpb_pad = jnp.pad(na_rpb, ((0, 0), (0, 0), (0, 0), (0, n_col_pad - n_col)))
    rpb_pairs = jnp.concatenate([rpb_pad[:, :, :-1], rpb_pad[:, :, 1:]], axis=-1)
    n_rows = DEPTH * NA_HEADS * n_pair
    row_block = 128
    pairs = pl.pallas_call(
        _toeplitz_kernel,
        out_shape=jax.ShapeDtypeStruct((n_rows, onehot.shape[1]), _f32),
        grid=(n_rows // row_block,),
        in_specs=[
            pl.BlockSpec((row_block, 2 * n_col_pad), lambda r: (r, 0)),
            pl.BlockSpec(onehot.shape, lambda r: (0, 0)),
            pl.BlockSpec(mask.shape, lambda r: (0, 0)),
        ],
        out_specs=pl.BlockSpec((row_block, onehot.shape[1]), lambda r: (r, 0)),
        compiler_params=pltpu.CompilerParams(dimension_semantics=("arbitrary",), vmem_limit_bytes=VMEM_LIMIT),
        name="rpb_toeplitz",
    )(rpb_pairs.reshape(n_rows, 2 * n_col_pad), jnp.asarray(onehot), jnp.asarray(mask))
    return pairs.reshape(DEPTH, N_HEAD_BLOCKS, HEADS_PER_BLOCK, n_pair, GRID_W, 2 * GRID_W)


def _rope_tables():
    t = jnp.arange(DEC_SEQ)
    half = NA_HEAD_DIM // 4
    freqs = ROPE_THETA ** (-jnp.arange(half, dtype=_f32) / half)
    ang_r = (t // GRID_W).astype(_f32)[:, None] * freqs[None, :]
    ang_c = (t % GRID_W).astype(_f32)[:, None] * freqs[None, :]
    cos_h = jnp.concatenate([jnp.cos(ang_r)] * 2 + [jnp.cos(ang_c)] * 2, axis=-1)
    sin_h = jnp.concatenate([-jnp.sin(ang_r), jnp.sin(ang_r), -jnp.sin(ang_c), jnp.sin(ang_c)], axis=-1)
    return jnp.tile(cos_h, (1, HEADS_PER_BLOCK)), jnp.tile(sin_h, (1, HEADS_PER_BLOCK))


IN_TM = 1024
IN_TN_F32_WEIGHT = 1024
IN_TN_BF16_WEIGHT = 1536
IN_PROJ_COLS = 2 * A_WIDTH + 3 * NA_WIDTH + 2 * D_MODEL
ROPE_COLS = (2 * A_WIDTH, 2 * A_WIDTH + 2 * NA_WIDTH)


def _in_kernel(tn, tile0, x_ref, shift_ref, scale_ref, g_ref, w_ref, cos_ref, sin_ref, *rest):
    o_ref, h_ref, ss_ref = rest[-3:]
    i = tile0 + pl.program_id(0)
    j = pl.program_id(1)

    @pl.when(j == 0)
    def _():
        _modnorm_to(x_ref, h_ref, ss_ref, IN_TM, g_ref, scale_ref, shift_ref)

    o_ref[...] = jnp.dot(h_ref[...], w_ref[0].astype(_bf16), preferred_element_type=_f32)

    for jt in range(IN_PROJ_COLS // tn):
        blocks = [cb for cb in range(tn // LANES) if ROPE_COLS[0] <= jt * tn + cb * LANES < ROPE_COLS[1]]
        if not blocks:
            continue

        @pl.when(jnp.logical_and(i >= N_CTX_TOK // IN_TM, j == jt))
        def _(blocks=blocks):
            lane = lax.broadcasted_iota(jnp.int32, (IN_TM, LANES), 1)
            first_half = (lane % (NA_HEAD_DIM // 2)) < (NA_HEAD_DIM // 4)
            cos = cos_ref[...]
            sin = sin_ref[...]
            for cb in blocks:
                cols = slice(cb * LANES, (cb + 1) * LANES)
                xb = o_ref[:, cols]
                partner = jnp.where(first_half, pltpu.roll(xb, LANES - NA_HEAD_DIM // 4, axis=1),
                                    pltpu.roll(xb, NA_HEAD_DIM // 4, axis=1))
                o_ref[:, cols] = xb * cos + partner * sin


def _mod_spec(layer, tm, chunk, tile0=0):
    return pl.BlockSpec((1, 1, 1, D_MODEL), lambda i, j: (layer, _mod_row(tile0 + i, tm), 0, chunk))


def _layer_vec_spec(layer, width):
    return pl.BlockSpec((1, 1, width), lambda *_: (layer, 0, 0))


def _in_proj(x, mod, g, w_in, w_layer, cos_t, sin_t, layer, tile0=0, proj=None):
    n_m = x.shape[0] // IN_TM
    tn = IN_TN_BF16_WEIGHT if w_in.dtype == _bf16 else IN_TN_F32_WEIGHT
    carried = () if proj is None else (proj,)
    return pl.pallas_call(
        functools.partial(_in_kernel, tn, tile0),
        out_shape=jax.ShapeDtypeStruct((N_TOK, IN_PROJ_COLS), _f32),
        grid=(n_m, IN_PROJ_COLS // tn),
        in_specs=[
            pl.BlockSpec((IN_TM, D_MODEL), lambda i, j: (i, 0)),
            _mod_spec(layer, IN_TM, 0, tile0), _mod_spec(layer, IN_TM, 1, tile0),
            _layer_vec_spec(layer, D_MODEL),
            pl.BlockSpec((1, D_MODEL, tn), lambda i, j: (w_layer, 0, j)),
            pl.BlockSpec((DEC_SEQ, LANES), lambda i, j: (0, 0)),
            pl.BlockSpec((DEC_SEQ, LANES), lambda i, j: (0, 0)),
        ] + [pl.BlockSpec(memory_space=pl.ANY)] * len(carried),
        out_specs=pl.BlockSpec((IN_TM, tn), lambda i, j: (tile0 + i, j)),
        scratch_shapes=[pltpu.VMEM((IN_TM, D_MODEL), _bf16), pltpu.VMEM((IN_TM, LANES), _f32)],
        input_output_aliases={7: 0} if carried else {},
        compiler_params=pltpu.CompilerParams(
            dimension_semantics=("arbitrary", "arbitrary"), vmem_limit_bytes=VMEM_LIMIT),
        name="in_proj",
    )(x, mod, mod, g, w_in, cos_t, sin_t, *carried)


SGU_TM = 1024


def _sgu_kernel(au_ref, av_ref, lng_ref, lnb_ref, ws_ref, bs_ref, o_ref):
    gdim = A_WIDTH // A_GROUPS

    def body(c, carry):
        rows = pl.ds(pl.multiple_of(c * CHUNK, CHUNK), CHUNK)
        u = _gelu(au_ref[rows, :])
        gv = _gelu(av_ref[rows, :])
        mu = jnp.mean(gv, axis=-1, keepdims=True)
        var = jnp.mean(jnp.square(gv - mu), axis=-1, keepdims=True)
        vn = ((gv - mu) * lax.rsqrt(var + EPS) * lng_ref[0] + lnb_ref[0]).astype(_bf16)
        for g in range(A_GROUPS):
            cols = slice(g * gdim, (g + 1) * gdim)
            s = jnp.dot(ws_ref[0, g], vn[:, cols], preferred_element_type=_f32) + bs_ref[0, g]
            o_ref[rows, cols] = (u[:, cols] * s).astype(_bf16)
        return carry

    lax.fori_loop(0, SGU_TM // CHUNK, body, 0)


def _sgu(proj, ln_g, ln_b, w_s, b_s, layer):
    gdim = A_WIDTH // A_GROUPS
    return pl.pallas_call(
        _sgu_kernel,
        out_shape=jax.ShapeDtypeStruct((N_TOK, A_WIDTH), _bf16),
        grid=(N_TOK // SGU_TM,),
        in_specs=[
            pl.BlockSpec((SGU_TM, A_WIDTH), lambda i: (i, 0)),
            pl.BlockSpec((SGU_TM, A_WIDTH), lambda i: (i, 1)),
            _layer_vec_spec(layer, A_WIDTH),
            _layer_vec_spec(layer, A_WIDTH),
            pl.BlockSpec((1, A_GROUPS, CHUNK, CHUNK), lambda i: (layer, 0, 0, 0)),
            pl.BlockSpec((1, A_GROUPS, CHUNK, gdim), lambda i: (layer, 0, 0, 0)),
        ],
        out_specs=pl.BlockSpec((SGU_TM, A_WIDTH), lambda i: (i, 0)),
        compiler_params=pltpu.CompilerParams(dimension_semantics=("arbitrary",), vmem_limit_bytes=VMEM_LIMIT),
        name="sgu",
    )(proj, proj, ln_g, ln_b, w_s, b_s)


ATT_SCALE = NA_HEAD_DIM ** -0.5
Q_LANE_BLOCK = (2 * A_WIDTH) // LANES
K_LANE_BLOCK = (2 * A_WIDTH + NA_WIDTH) // LANES
V_LANE_BLOCK = (2 * A_WIDTH + 2 * NA_WIDTH) // LANES


def _softmax_pv(s, v):
    e = jnp.exp(s - jnp.max(s, axis=-1, keepdims=True))
    return jnp.dot(e.astype(_bf16), v, preferred_element_type=_f32) / jnp.sum(e, axis=-1, keepdims=True)


CTX_SEQS_PER_BLOCK = 4
CTX_BLOCKS_PER_STEP = 2
CTX_SEQS_PER_STEP = CTX_SEQS_PER_BLOCK * CTX_BLOCKS_PER_STEP
CTX_TM = CTX_SEQS_PER_STEP * SEQ


def _head_lane_masks(rows):
    lane = lax.broadcasted_iota(jnp.int32, (rows, LANES), 1)
    return [jnp.logical_and(lane >= hh * NA_HEAD_DIM, lane < (hh + 1) * NA_HEAD_DIM)
            for hh in range(HEADS_PER_BLOCK)]


def _select_heads(masks, per_head):
    out = per_head[-1]
    for mask, val in zip(masks[:-1], per_head[:-1]):
        out = jnp.where(mask, val, out)
    return out


class _CastJob:
    def __init__(self, w, layer, n_slabs, split_rows, steps_per_slab=1):
        _, r, c = w.shape
        self.w, self.layer = w, layer
        self.out_shape = jax.ShapeDtypeStruct((r, c), _bf16)
        self.block = (r // n_slabs, c) if split_rows else (r, c // n_slabs)
        self.split_rows = split_rows
        self.steps_per_slab = steps_per_slab

    def specs(self, step_of):
        slab_of = lambda *g: step_of(*g) // self.steps_per_slab
        pos = (lambda *g: (slab_of(*g), 0)) if self.split_rows else (lambda *g: (0, slab_of(*g)))
        layer = self.layer
        return (pl.BlockSpec((1,) + self.block, lambda *g: (layer,) + pos(*g)), pl.BlockSpec(self.block, pos))


def _run_cast_jobs(src_refs, dst_refs):
    for src, dst in zip(src_refs, dst_refs):
        dst[...] = src[0].astype(_bf16)


def _ctx_attn_kernel(n_cast, q_ref, k_ref, v_ref, *rest):
    outs = rest[len(rest) - 3 - n_cast:]
    o_ref, kst_ref, vst_ref = outs[:3]
    _run_cast_jobs(rest[:n_cast], outs[3:])
    masks = _head_lane_masks(SEQ)
    block_rows = CTX_SEQS_PER_BLOCK * SEQ

    def block(blk, carry):
        seq_outs = []
        for s in range(CTX_SEQS_PER_BLOCK):
            rows = pl.ds(pl.multiple_of(blk * block_rows + s * SEQ, SEQ), SEQ)
            seq = blk * CTX_SEQS_PER_BLOCK + s
            q32 = q_ref[rows, :] * ATT_SCALE
            k32 = k_ref[rows, :]
            v32 = v_ref[rows, :]
            for hh in range(HEADS_PER_BLOCK):
                cols = slice(hh * NA_HEAD_DIM, (hh + 1) * NA_HEAD_DIM)
                kst_ref[seq, 0, hh] = k32[:, cols]
                vst_ref[seq, 0, hh] = v32[:, cols]
            kb = k32.astype(_bf16)
            vb = v32.astype(_bf16)
            per_head = []
            for mask in masks:
                qh = jnp.where(mask, q32, 0.0).astype(_bf16)
                sc = lax.dot_general(qh, kb, _NT, preferred_element_type=_f32)
                per_head.append(_softmax_pv(sc, vb))
            seq_outs.append(_select_heads(masks, per_head))
        out_rows = pl.ds(pl.multiple_of(blk * block_rows, block_rows), block_rows)
        o_ref[out_rows, :] = jnp.concatenate(seq_outs, axis=0).astype(_bf16)
        return carry

    lax.fori_loop(0, CTX_BLOCKS_PER_STEP, block, 0)


CTX_GRID = (N_CTX_TOK // CTX_TM, N_HEAD_BLOCKS)
CTX_STEPS = CTX_GRID[0] * CTX_GRID[1]


def _ctx_attention(proj, layer, states, cast_jobs):
    state_shape = jax.ShapeDtypeStruct((BATCH, DEPTH, NA_HEADS, SEQ, NA_HEAD_DIM), _f32)
    state_spec = pl.BlockSpec((CTX_SEQS_PER_STEP, 1, HEADS_PER_BLOCK, SEQ, NA_HEAD_DIM),
                              lambda i, h: (i, layer, h, 0, 0))
    cast_specs = [job.specs(lambda i, h: i * CTX_GRID[1] + h) for job in cast_jobs]
    n_cast, n_state_in = len(cast_jobs), len(states)
    return pl.pallas_call(
        functools.partial(_ctx_attn_kernel, n_cast),
        out_shape=(jax.ShapeDtypeStruct((N_TOK, NA_WIDTH), _bf16), state_shape, state_shape)
        + tuple(job.out_shape for job in cast_jobs),
        grid=CTX_GRID,
        in_specs=[
            pl.BlockSpec((CTX_TM, LANES), lambda i, h: (i, Q_LANE_BLOCK + h)),
            pl.BlockSpec((CTX_TM, LANES), lambda i, h: (i, K_LANE_BLOCK + h)),
            pl.BlockSpec((CTX_TM, LANES), lambda i, h: (i, V_LANE_BLOCK + h)),
        ] + [s[0] for s in cast_specs] + [pl.BlockSpec(memory_space=pl.ANY)] * n_state_in,
        out_specs=(pl.BlockSpec((CTX_TM, LANES), lambda i, h: (i, h)), state_spec, state_spec)
        + tuple(s[1] for s in cast_specs),
        input_output_aliases={3 + n_cast + n: 1 + n for n in range(n_state_in)},
        compiler_params=pltpu.CompilerParams(
            dimension_semantics=("arbitrary", "arbitrary"), vmem_limit_bytes=VMEM_LIMIT),
        name="ctx_attention",
    )(proj, proj, proj, *[job.w for job in cast_jobs], *states)


def _na_pair_bias(bias_ref, hh, blocks, lane_lo):
    pieces = []
    for a, keep_lo, keep_hi in blocks:
        if not (keep_lo or keep_hi):
            pieces.append(jnp.full((GRID_W, 2 * GRID_W), -jnp.inf, _f32))
            continue
        piece = bias_ref[0, 0, hh, a]
        if not keep_lo:
            piece = jnp.where(lane_lo, -jnp.inf, piece)
        if not keep_hi:
            piece = jnp.where(lane_lo, piece, -jnp.inf)
        pieces.append(piece)
    return jnp.concatenate(pieces, axis=-1)


def _na_attn_kernel(n_cast, q_ref, k_ref, v_ref, ck_ref, cv_ref, bias_ref, yb_hbm_ref, *rest):
    del yb_hbm_ref
    o_ref, e_refs = rest[n_cast], rest[-1]
    _run_cast_jobs(rest[:n_cast], rest[n_cast + 1:-1])
    lane_lo = lax.broadcasted_iota(jnp.int32, (GRID_W, 2 * GRID_W), 1) < GRID_W
    masks = _head_lane_masks(DEC_SEQ)
    q32 = q_ref[...] * ATT_SCALE
    ck = jnp.concatenate([ck_ref[0, 0, hh] for hh in range(HEADS_PER_BLOCK)], axis=-1)
    cv = jnp.concatenate([cv_ref[0, 0, hh] for hh in range(HEADS_PER_BLOCK)], axis=-1)
    keys = jnp.concatenate([k_ref[...], ck], axis=0).astype(_bf16)
    vals = jnp.concatenate([v_ref[...], cv], axis=0).astype(_bf16)
    ctx_cols = slice(DEC_SEQ, DEC_SEQ + PAST_LEN)
    head_outs = []
    for hh, mask in enumerate(masks):
        e_ref = e_refs.at[hh]
        e_ref[:, :DEC_SEQ] = jnp.zeros((DEC_SEQ, DEC_SEQ), _bf16)
        q = jnp.where(mask, q32, 0.0).astype(_bf16)
        s_all = lax.dot_general(q, keys, _NT, preferred_element_type=_f32)
        denoms = []
        for p, (w0, n_rows, per_row) in enumerate(NA_PLAN):
            qrows = slice(p * NA_QBLK, (p + 1) * NA_QBLK)
            kcols = slice(w0 * GRID_W, (w0 + n_rows) * GRID_W)
            bias = jnp.concatenate([_na_pair_bias(bias_ref, hh, blocks, lane_lo) for blocks in per_row], axis=0)
            s_win = s_all[qrows, kcols] + bias
            s_ctx = s_all[qrows, ctx_cols]
            m = jnp.maximum(jnp.max(s_win, axis=-1, keepdims=True), jnp.max(s_ctx, axis=-1, keepdims=True))
            e_win = jnp.exp(s_win - m)
            e_ctx = jnp.exp(s_ctx - m)
            denoms.append(jnp.sum(e_win, axis=-1, keepdims=True) + jnp.sum(e_ctx, axis=-1, keepdims=True))
            e_ref[qrows, kcols] = e_win.astype(_bf16)
            e_ref[qrows, ctx_cols] = e_ctx.astype(_bf16)
        half = DEC_SEQ // 2
        o = jnp.concatenate([jnp.dot(e_ref[r0:r0 + half, :], vals, preferred_element_type=_f32)
                             for r0 in (0, half)], axis=0)
        head_outs.append(o / jnp.concatenate(denoms, axis=0))
    o_ref[...] = _select_heads(masks, head_outs).astype(_bf16)


NA_GRID = (N_HEAD_BLOCKS, DEC_BATCH)
NA_STEPS = NA_GRID[0] * NA_GRID[1]


def _na_attention(proj, cache_k, cache_v, bias_tab, layer, yb, cast_jobs):
    tile0 = N_CTX_TOK // DEC_SEQ
    qkv_spec = lambda lane_block: pl.BlockSpec((DEC_SEQ, LANES), lambda h, b: (tile0 + b, lane_block + h))
    cache_spec = pl.BlockSpec((1, 1, HEADS_PER_BLOCK, PAST_LEN, NA_HEAD_DIM), lambda h, b: (b, layer, h, 0, 0))
    cast_specs = [job.specs(lambda h, b: h * NA_GRID[1] + b) for job in cast_jobs]
    return pl.pallas_call(
        functools.partial(_na_attn_kernel, len(cast_jobs)),
        out_shape=(jax.ShapeDtypeStruct((N_TOK, NA_WIDTH), _bf16),) + tuple(job.out_shape for job in cast_jobs),
        grid=NA_GRID,
        in_specs=[
            qkv_spec(Q_LANE_BLOCK), qkv_spec(K_LANE_BLOCK), qkv_spec(V_LANE_BLOCK),
            cache_spec, cache_spec,
            pl.BlockSpec((1, 1, HEADS_PER_BLOCK, NA_TOEP_ROWS - 1, GRID_W, 2 * GRID_W),
                         lambda h, b: (layer, h, 0, 0, 0, 0)),
            pl.BlockSpec(memory_space=pl.ANY),
        ] + [s[0] for s in cast_specs],
        out_specs=(pl.BlockSpec((DEC_SEQ, LANES), lambda h, b: (tile0 + b, h)),) + tuple(s[1] for s in cast_specs),
        scratch_shapes=[pltpu.VMEM((HEADS_PER_BLOCK, DEC_SEQ, DEC_SEQ + PAST_LEN), _bf16)],
        input_output_aliases={6: 0},
        compiler_params=pltpu.CompilerParams(
            dimension_semantics=("arbitrary", "arbitrary"), vmem_limit_bytes=VMEM_LIMIT),
        name="na_attention",
    )(proj, proj, proj, cache_k, cache_v, bias_tab, yb, *[job.w for job in cast_jobs])


MIX_TM = 512
MIX_TC = 1024
GA_COL_TILE = (2 * A_WIDTH + 3 * NA_WIDTH) // MIX_TC
GB_COL_TILE = GA_COL_TILE + D_MODEL // MIX_TC


def _mix_kernel(ga_ref, gb_ref, ya_ref, yb_ref, wpa_ref, wpb_ref, wout_ref, x_ref, gate_ref, g_ref, *rest):
    o_ref, ss_ref = rest[-2:]
    j = pl.program_id(1)

    @pl.when(j == 0)
    def _():
        o_ref[...] = jnp.zeros_like(o_ref)

    pa = jnp.dot(ya_ref[...], wpa_ref[...], preferred_element_type=_f32)
    pb = jnp.dot(yb_ref[...], wpb_ref[...], preferred_element_type=_f32)
    merged = (_sigmoid(ga_ref[...]) * pa + _sigmoid(gb_ref[...]) * pb).astype(_bf16)
    o_ref[...] += jnp.dot(merged, wout_ref[...], preferred_element_type=_f32)

    @pl.when(j == pl.num_programs(1) - 1)
    def _():
        _residual_rmsnorm_inplace(o_ref, x_ref, ss_ref, MIX_TM, gate_ref, g_ref)


def _mix(proj, ya, yb, w_pa, w_pb, w_out, x, mod, g_post, layer, tile0=0, x_out=None):
    carried = () if x_out is None else (x_out,)
    return pl.pallas_call(
        _mix_kernel,
        out_shape=jax.ShapeDtypeStruct((N_TOK, D_MODEL), _f32),
        grid=(x.shape[0] // MIX_TM, D_MODEL // MIX_TC),
        in_specs=[
            pl.BlockSpec((MIX_TM, MIX_TC), lambda i, j: (tile0 + i, GA_COL_TILE + j)),
            pl.BlockSpec((MIX_TM, MIX_TC), lambda i, j: (tile0 + i, GB_COL_TILE + j)),
            pl.BlockSpec((MIX_TM, A_WIDTH), lambda i, j: (tile0 + i, 0)),
            pl.BlockSpec((MIX_TM, NA_WIDTH), lambda i, j: (tile0 + i, 0)),
            pl.BlockSpec((A_WIDTH, MIX_TC), lambda i, j: (0, j)),
            pl.BlockSpec((NA_WIDTH, MIX_TC), lambda i, j: (0, j)),
            pl.BlockSpec((MIX_TC, D_MODEL), lambda i, j: (j, 0)),
            pl.BlockSpec((MIX_TM, D_MODEL), lambda i, j: (i, 0)),
            _mod_spec(layer, MIX_TM, 2, tile0),
            _layer_vec_spec(layer, D_MODEL),
        ] + [pl.BlockSpec(memory_space=pl.ANY)] * len(carried),
        out_specs=pl.BlockSpec((MIX_TM, D_MODEL), lambda i, j: (tile0 + i, 0)),
        scratch_shapes=[pltpu.VMEM((MIX_TM, LANES), _f32)],
        input_output_aliases={10: 0} if carried else {},
        compiler_params=pltpu.CompilerParams(
            dimension_semantics=("arbitrary", "arbitrary"), vmem_limit_bytes=VMEM_LIMIT),
        name="mix_out",
    )(proj, proj, ya, yb, w_pa, w_pb, w_out, x, mod, g_post, *carried)


FFN_TM = 1024
FFN_TF = 512


FFN_OUT_SPLIT = 2


def _ffn_kernel(n_cast, x_ref, shift_ref, scale_ref, gate_ref, gpre_ref, gpost_ref, w1_ref, w2_ref, *rest):
    o_ref, h_ref, ss_ref = rest[n_cast], rest[-2], rest[-1]
    _run_cast_jobs(rest[:n_cast], rest[n_cast + 1:-2])
    k = pl.program_id(1)

    @pl.when(k == 0)
    def _():
        def zero_out(rows, cols):
            o_ref[rows, cols] = jnp.zeros((NORM_ROWS, LANES), _f32)
        _modnorm_to(x_ref, h_ref, ss_ref, FFN_TM, gpre_ref, scale_ref, shift_ref, also=zero_out)

    a = jnp.dot(h_ref[...], w1_ref[...], preferred_element_type=_f32)
    a = jnp.square(jnp.maximum(a, 0.0)).astype(_bf16)
    width = D_MODEL // FFN_OUT_SPLIT
    for s in range(FFN_OUT_SPLIT):
        cols = slice(s * width, (s + 1) * width)
        o_ref[:, cols] += jnp.dot(a, w2_ref[:, cols], preferred_element_type=_f32)

    @pl.when(k == pl.num_programs(1) - 1)
    def _():
        _residual_rmsnorm_inplace(o_ref, x_ref, ss_ref, FFN_TM, gate_ref, gpost_ref)


FFN_K_STEPS = D_FF // FFN_TF
FFN_CAST_TILES = 8
FFN_CAST_STEPS = FFN_CAST_TILES * FFN_K_STEPS


def _ffn_cast_step(i, k):
    return jnp.where(i < FFN_CAST_TILES, i * FFN_K_STEPS + k, FFN_CAST_STEPS - 1)


def _ffn(x, mod, g_pre, g_post, w1, w2, layer, tile0=0, n_tiles=N_TOK // FFN_TM, cast_jobs=()):
    assert not cast_jobs or n_tiles >= FFN_CAST_TILES
    cast_specs = [job.specs(_ffn_cast_step) for job in cast_jobs]
    outs = pl.pallas_call(
        functools.partial(_ffn_kernel, len(cast_jobs)),
        out_shape=(jax.ShapeDtypeStruct((n_tiles * FFN_TM, D_MODEL), _f32),)
        + tuple(job.out_shape for job in cast_jobs),
        grid=(n_tiles, FFN_K_STEPS),
        in_specs=[
            pl.BlockSpec((FFN_TM, D_MODEL), lambda i, k: (tile0 + i, 0)),
            _mod_spec(layer, FFN_TM, 3, tile0), _mod_spec(layer, FFN_TM, 4, tile0), _mod_spec(layer, FFN_TM, 5, tile0),
            _layer_vec_spec(layer, D_MODEL),
            _layer_vec_spec(layer, D_MODEL),
            pl.BlockSpec((D_MODEL, FFN_TF), lambda i, k: (0, k)),
            pl.BlockSpec((FFN_TF, D_MODEL), lambda i, k: (k, 0)),
        ] + [s[0] for s in cast_specs],
        out_specs=(pl.BlockSpec((FFN_TM, D_MODEL), lambda i, k: (i, 0)),) + tuple(s[1] for s in cast_specs),
        scratch_shapes=[pltpu.VMEM((FFN_TM, D_MODEL), _bf16), pltpu.VMEM((FFN_TM, LANES), _f32)],
        compiler_params=pltpu.CompilerParams(
            dimension_semantics=("arbitrary", "arbitrary"), vmem_limit_bytes=VMEM_LIMIT),
        name="ffn",
    )(x, mod, mod, mod, g_pre, g_post, w1, w2, *[job.w for job in cast_jobs])
    return outs


def kernel(x_prompt, x_sample, cache_ctx_k, cache_ctx_v, c, c_ctx, w_mod, b_mod, g_pre_mix, g_post_mix, g_pre_ffn,
           g_post_ffn, w_in, sgu_ln_g, sgu_ln_b, sgu_w, sgu_b, na_rpb, w_pa, w_pb, w_out, w_ff1, w_ff2):
    x_streams = (x_prompt.reshape(N_CTX_TOK, D_MODEL), x_sample.reshape(DEC_BATCH * DEC_SEQ, D_MODEL))
    cvec =jnp.concatenate([c_ctx[None, :], c, jnp.zeros((MOD_ROWS - 1 - DEC_BATCH, D_MODEL), _f32)], axis=0)
    mod = _modulation(cvec, w_mod, b_mod).reshape(DEPTH, MOD_ROWS, 1, N_MOD * D_MODEL)
    bias = _na_bias_tables(na_rpb)
    cos_t, sin_t = _rope_tables()
    gdim = A_WIDTH // A_GROUPS
    sgu_b_lanes = jnp.broadcast_to(sgu_b[:, :, :, None], (DEPTH, A_GROUPS, CHUNK, gdim))

    vec = lambda a: a[:, None, :]
    g_pre_mix, g_post_mix, g_pre_ffn, g_post_ffn = vec(g_pre_mix), vec(g_post_mix), vec(g_pre_ffn), vec(g_post_ffn)
    sgu_ln_g, sgu_ln_b = vec(sgu_ln_g), vec(sgu_ln_b)
    sgu_w = sgu_w.astype(_bf16)

    states = ()
    for l in range(DEPTH):
        if l == 0:
            w_in_l = w_in
            ctx_jobs = [_CastJob(w_ff1, 0, CTX_STEPS, split_rows=False), _CastJob(w_ff2, 0, CTX_STEPS, split_rows=True)]
            na_jobs = [_CastJob(w, 0, NA_STEPS, split_rows=True) for w in (w_pa, w_pb, w_out)]
        else:
            w_in_l, ctx_jobs, na_jobs = w_in_b[None], [], []
        streams = x_streams if l == 0 else (x,)
        proj, row0 = None, 0
        for xs in streams:
            proj = _in_proj(xs, mod, g_pre_mix, w_in_l, 0, cos_t, sin_t, l, tile0=row0 // IN_TM, proj=proj)
            row0 += xs.shape[0]
        ya = _sgu(proj, sgu_ln_g, sgu_ln_b, sgu_w, sgu_b_lanes, l)
        yb, k_state, v_state, *ctx_cast = _ctx_attention(proj, l, states, ctx_jobs)
        states = (k_state, v_state)
        yb, *na_cast = _na_attention(proj, cache_ctx_k, cache_ctx_v, bias, l, yb, na_jobs)
        if l == 0:
            (w_ff1_b, w_ff2_b), (w_pa_b, w_pb_b, w_out_b) = ctx_cast, na_cast
        x, row0 = None, 0
        for xs in streams:
            x = _mix(proj, ya, yb, w_pa_b, w_pb_b, w_out_b, xs, mod, g_post_mix, l, tile0=row0 // MIX_TM, x_out=x)
            row0 += xs.shape[0]
        if l + 1 < DEPTH:
            nxt = l + 1
            jobs = [_CastJob(w, nxt, FFN_CAST_STEPS, split_rows=True) for w in (w_in, w_ff1, w_ff2, w_out)]
            jobs += [_CastJob(w, nxt, FFN_CAST_STEPS // 2, split_rows=True, steps_per_slab=2) for w in (w_pa, w_pb)]
            x, w_in_b, w_ff1_b, w_ff2_b, w_out_b, w_pa_b, w_pb_b = _ffn(
                x, mod, g_pre_ffn, g_post_ffn, w_ff1_b, w_ff2_b, l, cast_jobs=jobs)
        else:
            n_ctx_tiles = N_CTX_TOK // FFN_TM
            last = functools.partial(_ffn, x, mod, g_pre_ffn, g_post_ffn, w_ff1_b, w_ff2_b, l)
            y_prompt, = last(tile0=0, n_tiles=n_ctx_tiles)
            y_sample, = last(tile0=n_ctx_tiles, n_tiles=N_TOK // FFN_TM - n_ctx_tiles)
    return (y_prompt.reshape(BATCH, SEQ, D_MODEL), y_sample.reshape(DEC_BATCH, DEC_SEQ, D_MODEL), states[0], states[1])
```

```python
import functools

import numpy as np
import jax
import jax.numpy as jnp
from jax import lax
from jax.experimental import pallas as pl
from jax.experimental.pallas import tpu as pltpu

D_MODEL = 2048
BATCH = 32
SEQ = 256
DEPTH = 4
DEC_BATCH = 2
DEC_SEQ = 1024
PAST_LEN = 512
GRID_W = 64
CHUNK = 128
A_WIDTH = 1024
A_GROUPS = 8
NA_HEADS = 16
NA_HEAD_DIM = 64
NA_WIDTH = NA_HEADS * NA_HEAD_DIM
NA_KH_MAX = 8
NA_KW = 16
D_FF = 4 * D_MODEL
ROPE_THETA = 10000.0
EPS = 1e-6
N_MOD = 6
IN_COLS = 2 * A_WIDTH + 3 * NA_WIDTH + 2 * D_MODEL

N_CTX_TOK = BATCH * SEQ
N_TOK = N_CTX_TOK + DEC_BATCH * DEC_SEQ
ROWS = DEC_SEQ // GRID_W
NA_KH = min(NA_KH_MAX, ROWS)
MOD_ROWS = 8
LANES = 128
HEADS_PER_BLOCK = LANES // NA_HEAD_DIM
N_HEAD_BLOCKS = NA_HEADS // HEADS_PER_BLOCK

NA_QROWS = 2
NA_QBLK = NA_QROWS * GRID_W
NA_NPAIR = ROWS // NA_QROWS
NA_TOEP_ROWS = 2 * NA_KH_MAX - 1

VMEM_LIMIT = 56 * 1024 * 1024

_f32 = jnp.float32
_bf16 = jnp.bfloat16
_NT = (((1,), (1,)), ((), ()))


def _na_window_plan():
    plan = []
    for p in range(NA_NPAIR):
        rows = [NA_QROWS * p + rr for rr in range(NA_QROWS)]
        starts = [int(np.clip(r - NA_KH // 2, 0, ROWS - NA_KH)) for r in rows]
        w0 = min(starts) // 2 * 2
        w1 = -(-(max(starts) + NA_KH) // 2) * 2
        per_row = []
        for r, rs in zip(rows, starts):
            blocks = []
            for i in range(w0, w1, 2):
                keep_lo = rs <= i < rs + NA_KH
                keep_hi = rs <= i + 1 < rs + NA_KH
                a = i - r + NA_KH_MAX - 1
                assert not (keep_lo or keep_hi) or 0 <= a <= NA_TOEP_ROWS - 2
                blocks.append((a, keep_lo, keep_hi))
            per_row.append(blocks)
        plan.append((w0, w1 - w0, per_row))
    return plan


NA_PLAN = _na_window_plan()


def _mod_row(tile, tm):
    return jnp.maximum((tile * tm - N_CTX_TOK) // DEC_SEQ + 1, 0)


def _sigmoid(x):
    return 1.0 / (1.0 + jnp.exp(-x))


def _gelu(x):
    return 0.5 * x * (1.0 + jnp.tanh(np.sqrt(2.0 / np.pi).astype(np.float32) * (x + 0.044715 * (x * x * x))))


SUMSQ_ROWS = 256
NORM_ROWS = 64
N_LANE_BLOCKS = D_MODEL // LANES


def _row_sumsq_pass(src_ref, ss_ref, n_rows):
    def body(c, carry):
        rows = pl.ds(pl.multiple_of(c * SUMSQ_ROWS, SUMSQ_ROWS), SUMSQ_ROWS)
        acc = jnp.zeros((SUMSQ_ROWS, LANES), _f32)
        for cb in range(N_LANE_BLOCKS):
            v = src_ref[rows, cb * LANES:(cb + 1) * LANES]
            acc = acc + v * v
        ss_ref[rows, :] = jnp.broadcast_to(jnp.sum(acc, axis=-1, keepdims=True), (SUMSQ_ROWS, LANES))
        return carry
    lax.fori_loop(0, n_rows // SUMSQ_ROWS, body, 0)


def _row_rescale_pass(ss_ref, n_rows, block_fn):
    def body(c, carry):
        rows = pl.ds(pl.multiple_of(c * NORM_ROWS, NORM_ROWS), NORM_ROWS)
        r = lax.rsqrt(ss_ref[rows, :] * (1.0 / D_MODEL) + EPS)
        for cb in range(N_LANE_BLOCKS):
            block_fn(rows, slice(cb * LANES, (cb + 1) * LANES), r)
        return carry
    lax.fori_loop(0, n_rows // NORM_ROWS, body, 0)


def _modnorm_to(x_ref, h_ref, ss_ref, n_rows, g_ref, scale_ref, shift_ref, also=None):
    _row_sumsq_pass(x_ref, ss_ref, n_rows)

    def block(rows, cols, r):
        y = x_ref[rows, cols] * r
        h = (y * g_ref[0, :, cols]) * (1.0 + scale_ref[0, 0, :, cols]) + shift_ref[0, 0, :, cols]
        h_ref[rows, cols] = h.astype(_bf16)
        if also is not None:
            also(rows, cols)
    _row_rescale_pass(ss_ref, n_rows, block)


def _residual_rmsnorm_inplace(o_ref, x_ref, ss_ref, n_rows, gate_ref, g_ref):
    _row_sumsq_pass(o_ref, ss_ref, n_rows)

    def block(rows, cols, r):
        y = (o_ref[rows, cols] * r) * g_ref[0, :, cols]
        o_ref[rows, cols] = x_ref[rows, cols] + gate_ref[0, 0, :, cols] * y
    _row_rescale_pass(ss_ref, n_rows, block)


def _mod_kernel(c_ref, w_ref, b_ref, o_ref):
    c = c_ref[...]
    s = (c * _sigmoid(c)).astype(_bf16)
    o_ref[0] = jnp.dot(s, w_ref[0].astype(_bf16), preferred_element_type=_f32) + b_ref[0]


def _modulation(cvec, w_mod, b_mod):
    tn = D_MODEL
    n_cols = N_MOD * D_MODEL
    return pl.pallas_call(
        _mod_kernel,
        out_shape=jax.ShapeDtypeStruct((DEPTH, MOD_ROWS, n_cols), _f32),
        grid=(DEPTH, n_cols // tn),
        in_specs=[
            pl.BlockSpec((MOD_ROWS, D_MODEL), lambda l, j: (0, 0)),
            pl.BlockSpec((1, D_MODEL, tn), lambda l, j: (l, 0, j)),
            pl.BlockSpec((1, 1, tn), lambda l, j: (l, 0, j)),
        ],
        out_specs=pl.BlockSpec((1, MOD_ROWS, tn), lambda l, j: (l, 0, j)),
        compiler_params=pltpu.CompilerParams(
            dimension_semantics=("arbitrary", "arbitrary"), vmem_limit_bytes=VMEM_LIMIT),
        name="modulation",
    )(cvec, w_mod, b_mod.reshape(DEPTH, 1, n_cols))


def _toeplitz_kernel(rpb_ref, onehot_ref, mask_ref, o_ref):
    o_ref[...] = jnp.dot(rpb_ref[...], onehot_ref[...], preferred_element_type=_f32,
                         precision=lax.Precision.HIGHEST) + mask_ref[...]


def _na_bias_tables(na_rpb):
    n_pair = NA_TOEP_ROWS - 1
    n_col = 2 * NA_KW - 1
    n_col_pad = 32
    cols = np.arange(GRID_W)
    cstart = np.clip(cols - NA_KW // 2, 0, GRID_W - NA_KW)
    in_win = (cols[None, :] >= cstart[:, None]) & (cols[None, :] < cstart[:, None] + NA_KW)
    col_idx = np.clip(cols[None, :] - cols[:, None] + NA_KW - 1, 0, n_col - 1)
    onehot = np.zeros((2, n_col_pad, GRID_W, 2, GRID_W), np.float32)
    for half in range(2):
        onehot[half, :, :, half, :] = (col_idx[None] == np.arange(n_col_pad)[:, None, None]) & in_win[None]
    onehot = onehot.reshape(2 * n_col_pad, 2 * GRID_W * GRID_W)
    mask = np.where(np.broadcast_to(in_win[:, None, :], (GRID_W, 2, GRID_W)), 0.0, -np.inf)
    mask = mask.reshape(1, -1).astype(np.float32)
    rpb_pad = jnp.pad(na_rpb, ((0, 0), (0, 0), (0, 0), (0, n_col_pad - n_col)))
    rpb_pairs = jnp.concatenate([rpb_pad[:, :, :-1], rpb_pad[:, :, 1:]], axis=-1)
    n_rows = DEPTH * NA_HEADS * n_pair
    row_block = 128
    pairs = pl.pallas_call(
        _toeplitz_kernel,
        out_shape=jax.ShapeDtypeStruct((n_rows, onehot.shape[1]), _f32),
        grid=(n_rows // row_block,),
        in_specs=[
            pl.BlockSpec((row_block, 2 * n_col_pad), lambda r: (r, 0)),
            pl.BlockSpec(onehot.shape, lambda r: (0, 0)),
            pl.BlockSpec(mask.shape, lambda r: (0, 0)),
        ],
        out_specs=pl.BlockSpec((row_block, onehot.shape[1]), lambda r: (r, 0)),
        compiler_params=pltpu.CompilerParams(dimension_semantics=("arbitrary",), vmem_limit_bytes=VMEM_LIMIT),
        name="rpb_toeplitz",
    )(rpb_pairs.reshape(n_rows, 2 * n_col_pad), jnp.asarray(onehot), jnp.asarray(mask))
    return pairs.reshape(DEPTH, N_HEAD_BLOCKS, HEADS_PER_BLOCK, n_pair, GRID_W, 2 * GRID_W)


def _rope_tables():
    t = jnp.arange(DEC_SEQ)
    half = NA_HEAD_DIM // 4
    freqs = ROPE_THETA ** (-jnp.arange(half, dtype=_f32) / half)
    ang_r = (t // GRID_W).astype(_f32)[:, None] * freqs[None, :]
    ang_c = (t % GRID_W).astype(_f32)[:, None] * freqs[None, :]
    cos_h = jnp.concatenate([jnp.cos(ang_r)] * 2 + [jnp.cos(ang_c)] * 2, axis=-1)
    sin_h = jnp.concatenate([-jnp.sin(ang_r), jnp.sin(ang_r), -jnp.sin(ang_c), jnp.sin(ang_c)], axis=-1)
    return jnp.tile(cos_h, (1, HEADS_PER_BLOCK)), jnp.tile(sin_h, (1, HEADS_PER_BLOCK))


IN_TM = 1024
IN_TN_F32_WEIGHT = 1024
IN_TN_BF16_WEIGHT = 1536
IN_PROJ_COLS = 2 * A_WIDTH + 3 * NA_WIDTH + 2 * D_MODEL
ROPE_COLS = (2 * A_WIDTH, 2 * A_WIDTH + 2 * NA_WIDTH)


def _in_kernel(tn, tile0, x_ref, shift_ref, scale_ref, g_ref, w_ref, cos_ref, sin_ref, *rest):
    o_ref, h_ref, ss_ref = rest[-3:]
    i = tile0 + pl.program_id(0)
    j = pl.program_id(1)

    @pl.when(j == 0)
    def _():
        _modnorm_to(x_ref, h_ref, ss_ref, IN_TM, g_ref, scale_ref, shift_ref)

    o_ref[...] = jnp.dot(h_ref[...], w_ref[0].astype(_bf16), preferred_element_type=_f32)

    for jt in range(IN_PROJ_COLS // tn):
        blocks = [cb for cb in range(tn // LANES) if ROPE_COLS[0] <= jt * tn + cb * LANES < ROPE_COLS[1]]
        if not blocks:
            continue

        @pl.when(jnp.logical_and(i >= N_CTX_TOK // IN_TM, j == jt))
        def _(blocks=blocks):
            lane = lax.broadcasted_iota(jnp.int32, (IN_TM, LANES), 1)
            first_half = (lane % (NA_HEAD_DIM // 2)) < (NA_HEAD_DIM // 4)
            cos = cos_ref[...]
            sin = sin_ref[...]
            for cb in blocks:
                cols = slice(cb * LANES, (cb + 1) * LANES)
                xb = o_ref[:, cols]
                partner = jnp.where(first_half, pltpu.roll(xb, LANES - NA_HEAD_DIM // 4, axis=1),
                                    pltpu.roll(xb, NA_HEAD_DIM // 4, axis=1))
                o_ref[:, cols] = xb * cos + partner * sin


def _mod_spec(layer, tm, chunk, tile0=0):
    return pl.BlockSpec((1, 1, 1, D_MODEL), lambda i, j: (layer, _mod_row(tile0 + i, tm), 0, chunk))


def _layer_vec_spec(layer, width):
    return pl.BlockSpec((1, 1, width), lambda *_: (layer, 0, 0))


def _in_proj(x, mod, g, w_in, w_layer, cos_t, sin_t, layer, tile0=0, proj=None):
    n_m = x.shape[0] // IN_TM
    tn = IN_TN_BF16_WEIGHT if w_in.dtype == _bf16 else IN_TN_F32_WEIGHT
    carried = () if proj is None else (proj,)
    return pl.pallas_call(
        functools.partial(_in_kernel, tn, tile0),
        out_shape=jax.ShapeDtypeStruct((N_TOK, IN_PROJ_COLS), _f32),
        grid=(n_m, IN_PROJ_COLS // tn),
        in_specs=[
            pl.BlockSpec((IN_TM, D_MODEL), lambda i, j: (i, 0)),
            _mod_spec(layer, IN_TM, 0, tile0), _mod_spec(layer, IN_TM, 1, tile0),
            _layer_vec_spec(layer, D_MODEL),
            pl.BlockSpec((1, D_MODEL, tn), lambda i, j: (w_layer, 0, j)),
            pl.BlockSpec((DEC_SEQ, LANES), lambda i, j: (0, 0)),
            pl.BlockSpec((DEC_SEQ, LANES), lambda i, j: (0, 0)),
        ] + [pl.BlockSpec(memory_space=pl.ANY)] * len(carried),
        out_specs=pl.BlockSpec((IN_TM, tn), lambda i, j: (tile0 + i, j)),
        scratch_shapes=[pltpu.VMEM((IN_TM, D_MODEL), _bf16), pltpu.VMEM((IN_TM, LANES), _f32)],
        input_output_aliases={7: 0} if carried else {},
        compiler_params=pltpu.CompilerParams(
            dimension_semantics=("arbitrary", "arbitrary"), vmem_limit_bytes=VMEM_LIMIT),
        name="in_proj",
    )(x, mod, mod, g, w_in, cos_t, sin_t, *carried)


SGU_TM = 1024


def _sgu_kernel(au_ref, av_ref, lng_ref, lnb_ref, ws_ref, bs_ref, o_ref):
    gdim = A_WIDTH // A_GROUPS

    def body(c, carry):
        rows = pl.ds(pl.multiple_of(c * CHUNK, CHUNK), CHUNK)
        u = _gelu(au_ref[rows, :])
        gv = _gelu(av_ref[rows, :])
        mu = jnp.mean(gv, axis=-1, keepdims=True)
        var = jnp.mean(jnp.square(gv - mu), axis=-1, keepdims=True)
        vn = ((gv - mu) * lax.rsqrt(var + EPS) * lng_ref[0] + lnb_ref[0]).astype(_bf16)
        for g in range(A_GROUPS):
            cols = slice(g * gdim, (g + 1) * gdim)
            s = jnp.dot(ws_ref[0, g], vn[:, cols], preferred_element_type=_f32) + bs_ref[0, g]
            o_ref[rows, cols] = (u[:, cols] * s).astype(_bf16)
        return carry

    lax.fori_loop(0, SGU_TM // CHUNK, body, 0)


def _sgu(proj, ln_g, ln_b, w_s, b_s, layer):
    gdim = A_WIDTH // A_GROUPS
    return pl.pallas_call(
        _sgu_kernel,
        out_shape=jax.ShapeDtypeStruct((N_TOK, A_WIDTH), _bf16),
        grid=(N_TOK // SGU_TM,),
        in_specs=[
            pl.BlockSpec((SGU_TM, A_WIDTH), lambda i: (i, 0)),
            pl.BlockSpec((SGU_TM, A_WIDTH), lambda i: (i, 1)),
            _layer_vec_spec(layer, A_WIDTH),
            _layer_vec_spec(layer, A_WIDTH),
            pl.BlockSpec((1, A_GROUPS, CHUNK, CHUNK), lambda i: (layer, 0, 0, 0)),
            pl.BlockSpec((1, A_GROUPS, CHUNK, gdim), lambda i: (layer, 0, 0, 0)),
        ],
        out_specs=pl.BlockSpec((SGU_TM, A_WIDTH), lambda i: (i, 0)),
        compiler_params=pltpu.CompilerParams(dimension_semantics=("arbitrary",), vmem_limit_bytes=VMEM_LIMIT),
        name="sgu",
    )(proj, proj, ln_g, ln_b, w_s, b_s)


ATT_SCALE = NA_HEAD_DIM ** -0.5
Q_LANE_BLOCK = (2 * A_WIDTH) // LANES
K_LANE_BLOCK = (2 * A_WIDTH + NA_WIDTH) // LANES
V_LANE_BLOCK = (2 * A_WIDTH + 2 * NA_WIDTH) // LANES


def _softmax_pv(s, v):
    e = jnp.exp(s - jnp.max(s, axis=-1, keepdims=True))
    return jnp.dot(e.astype(_bf16), v, preferred_element_type=_f32) / jnp.sum(e, axis=-1, keepdims=True)


CTX_SEQS_PER_BLOCK = 4
CTX_BLOCKS_PER_STEP = 2
CTX_SEQS_PER_STEP = CTX_SEQS_PER_BLOCK * CTX_BLOCKS_PER_STEP
CTX_TM = CTX_SEQS_PER_STEP * SEQ


def _head_lane_masks(rows):
    lane = lax.broadcasted_iota(jnp.int32, (rows, LANES), 1)
    return [jnp.logical_and(lane >= hh * NA_HEAD_DIM, lane < (hh + 1) * NA_HEAD_DIM)
            for hh in range(HEADS_PER_BLOCK)]


def _select_heads(masks, per_head):
    out = per_head[-1]
    for mask, val in zip(masks[:-1], per_head[:-1]):
        out = jnp.where(mask, val, out)
    return out


class _CastJob:
    def __init__(self, w, layer, n_slabs, split_rows, steps_per_slab=1):
        _, r, c = w.shape
        self.w, self.layer = w, layer
        self.out_shape = jax.ShapeDtypeStruct((r, c), _bf16)
        self.block = (r // n_slabs, c) if split_rows else (r, c // n_slabs)
        self.split_rows = split_rows
        self.steps_per_slab = steps_per_slab

    def specs(self, step_of):
        slab_of = lambda *g: step_of(*g) // self.steps_per_slab
        pos = (lambda *g: (slab_of(*g), 0)) if self.split_rows else (lambda *g: (0, slab_of(*g)))
        layer = self.layer
        return (pl.BlockSpec((1,) + self.block, lambda *g: (layer,) + pos(*g)), pl.BlockSpec(self.block, pos))


def _run_cast_jobs(src_refs, dst_refs):
    for src, dst in zip(src_refs, dst_refs):
        dst[...] = src[0].astype(_bf16)


def _ctx_attn_kernel(n_cast, q_ref, k_ref, v_ref, *rest):
    outs = rest[len(rest) - 3 - n_cast:]
    o_ref, kst_ref, vst_ref = outs[:3]
    _run_cast_jobs(rest[:n_cast], outs[3:])
    masks = _head_lane_masks(SEQ)
    block_rows = CTX_SEQS_PER_BLOCK * SEQ

    def block(blk, carry):
        seq_outs = []
        for s in range(CTX_SEQS_PER_BLOCK):
            rows = pl.ds(pl.multiple_of(blk * block_rows + s * SEQ, SEQ), SEQ)
            seq = blk * CTX_SEQS_PER_BLOCK + s
            q32 = q_ref[rows, :] * ATT_SCALE
            k32 = k_ref[rows, :]
            v32 = v_ref[rows, :]
            kst_ref[seq, 0] = k32.T.reshape(HEADS_PER_BLOCK, NA_HEAD_DIM, SEQ)
            vst_ref[seq, 0] = v32.T.reshape(HEADS_PER_BLOCK, NA_HEAD_DIM, SEQ)
            kb = k32.astype(_bf16)
            vb = v32.astype(_bf16)
            per_head = []
            for mask in masks:
                qh = jnp.where(mask, q32, 0.0).astype(_bf16)
                sc = lax.dot_general(qh, kb, _NT, preferred_element_type=_f32)
                per_head.append(_softmax_pv(sc, vb))
            seq_outs.append(_select_heads(masks, per_head))
        out_rows = pl.ds(pl.multiple_of(blk * block_rows, block_rows), block_rows)
        o_ref[out_rows, :] = jnp.concatenate(seq_outs, axis=0).astype(_bf16)
        return carry

    lax.fori_loop(0, CTX_BLOCKS_PER_STEP, block, 0)


CTX_GRID = (N_CTX_TOK // CTX_TM, N_HEAD_BLOCKS)
CTX_STEPS = CTX_GRID[0] * CTX_GRID[1]


def _ctx_attention(proj, layer, states, cast_jobs):
    state_shape = jax.ShapeDtypeStruct((BATCH, DEPTH, NA_HEADS, NA_HEAD_DIM, SEQ), _f32)
    state_spec = pl.BlockSpec((CTX_SEQS_PER_STEP, 1, HEADS_PER_BLOCK, NA_HEAD_DIM, SEQ),
                              lambda i, h: (i, layer, h, 0, 0))
    cast_specs = [job.specs(lambda i, h: i * CTX_GRID[1] + h) for job in cast_jobs]
    n_cast, n_state_in = len(cast_jobs), len(states)
    return pl.pallas_call(
        functools.partial(_ctx_attn_kernel, n_cast),
        out_shape=(jax.ShapeDtypeStruct((N_TOK, NA_WIDTH), _bf16), state_shape, state_shape)
        + tuple(job.out_shape for job in cast_jobs),
        grid=CTX_GRID,
        in_specs=[
            pl.BlockSpec((CTX_TM, LANES), lambda i, h: (i, Q_LANE_BLOCK + h)),
            pl.BlockSpec((CTX_TM, LANES), lambda i, h: (i, K_LANE_BLOCK + h)),
            pl.BlockSpec((CTX_TM, LANES), lambda i, h: (i, V_LANE_BLOCK + h)),
        ] + [s[0] for s in cast_specs] + [pl.BlockSpec(memory_space=pl.ANY)] * n_state_in,
        out_specs=(pl.BlockSpec((CTX_TM, LANES), lambda i, h: (i, h)), state_spec, state_spec)
        + tuple(s[1] for s in cast_specs),
        input_output_aliases={3 + n_cast + n: 1 + n for n in range(n_state_in)},
        compiler_params=pltpu.CompilerParams(
            dimension_semantics=("arbitrary", "arbitrary"), vmem_limit_bytes=VMEM_LIMIT),
        name="ctx_attention",
    )(proj, proj, proj, *[job.w for job in cast_jobs], *states)


def _na_pair_bias(bias_ref, hh, blocks, lane_lo):
    pieces = []
    for a, keep_lo, keep_hi in blocks:
        if not (keep_lo or keep_hi):
            pieces.append(jnp.full((GRID_W, 2 * GRID_W), -jnp.inf, _f32))
            continue
        piece = bias_ref[0, 0, hh, a]
        if not keep_lo:
            piece = jnp.where(lane_lo, -jnp.inf, piece)
        if not keep_hi:
            piece = jnp.where(lane_lo, piece, -jnp.inf)
        pieces.append(piece)
    return jnp.concatenate(pieces, axis=-1)


def _na_attn_kernel(n_cast, q_ref, k_ref, v_ref, ck_ref, cv_ref, bias_ref, yb_hbm_ref, *rest):
    del yb_hbm_ref
    o_ref, e_refs = rest[n_cast], rest[-1]
    _run_cast_jobs(rest[:n_cast], rest[n_cast + 1:-1])
    lane_lo = lax.broadcasted_iota(jnp.int32, (GRID_W, 2 * GRID_W), 1) < GRID_W
    masks = _head_lane_masks(DEC_SEQ)
    q32 = q_ref[...] * ATT_SCALE
    ck = jnp.concatenate([ck_ref[0, 0, hh] for hh in range(HEADS_PER_BLOCK)], axis=-1)
    cv = jnp.concatenate([cv_ref[0, 0, hh] for hh in range(HEADS_PER_BLOCK)], axis=-1)
    keys = jnp.concatenate([k_ref[...], ck], axis=0).astype(_bf16)
    vals = jnp.concatenate([v_ref[...], cv], axis=0).astype(_bf16)
    ctx_cols = slice(DEC_SEQ, DEC_SEQ + PAST_LEN)
    head_outs = []
    for hh, mask in enumerate(masks):
        e_ref = e_refs.at[hh]
        e_ref[:, :DEC_SEQ] = jnp.zeros((DEC_SEQ, DEC_SEQ), _bf16)
        q = jnp.where(mask, q32, 0.0).astype(_bf16)
        s_all = lax.dot_general(q, keys, _NT, preferred_element_type=_f32)
        denoms = []
        for p, (w0, n_rows, per_row) in enumerate(NA_PLAN):
            qrows = slice(p * NA_QBLK, (p + 1) * NA_QBLK)
            kcols = slice(w0 * GRID_W, (w0 + n_rows) * GRID_W)
            bias = jnp.concatenate([_na_pair_bias(bias_ref, hh, blocks, lane_lo) for blocks in per_row], axis=0)
            s_win = s_all[qrows, kcols] + bias
            s_ctx = s_all[qrows, ctx_cols]
            m = jnp.maximum(jnp.max(s_win, axis=-1, keepdims=True), jnp.max(s_ctx, axis=-1, keepdims=True))
            e_win = jnp.exp(s_win - m)
            e_ctx = jnp.exp(s_ctx - m)
            denoms.append(jnp.sum(e_win, axis=-1, keepdims=True) + jnp.sum(e_ctx, axis=-1, keepdims=True))
            e_ref[qrows, kcols] = e_win.astype(_bf16)
            e_ref[qrows, ctx_cols] = e_ctx.astype(_bf16)
        half = DEC_SEQ // 2
        o = jnp.concatenate([jnp.dot(e_ref[r0:r0 + half, :], vals, preferred_element_type=_f32)
                             for r0 in (0, half)], axis=0)
        head_outs.append(o / jnp.concatenate(denoms, axis=0))
    o_ref[...] = _select_heads(masks, head_outs).astype(_bf16)


NA_GRID = (N_HEAD_BLOCKS, DEC_BATCH)
NA_STEPS = NA_GRID[0] * NA_GRID[1]


def _na_attention(proj, cache_k, cache_v, bias_tab, layer, yb, cast_jobs):
    tile0 = N_CTX_TOK // DEC_SEQ
    qkv_spec = lambda lane_block: pl.BlockSpec((DEC_SEQ, LANES), lambda h, b: (tile0 + b, lane_block + h))
    cache_spec = pl.BlockSpec((1, 1, HEADS_PER_BLOCK, PAST_LEN, NA_HEAD_DIM), lambda h, b: (b, layer, h, 0, 0))
    cast_specs = [job.specs(lambda h, b: h * NA_GRID[1] + b) for job in cast_jobs]
    return pl.pallas_call(
        functools.partial(_na_attn_kernel, len(cast_jobs)),
        out_shape=(jax.ShapeDtypeStruct((N_TOK, NA_WIDTH), _bf16),) + tuple(job.out_shape for job in cast_jobs),
        grid=NA_GRID,
        in_specs=[
            qkv_spec(Q_LANE_BLOCK), qkv_spec(K_LANE_BLOCK), qkv_spec(V_LANE_BLOCK),
            cache_spec, cache_spec,
            pl.BlockSpec((1, 1, HEADS_PER_BLOCK, NA_TOEP_ROWS - 1, GRID_W, 2 * GRID_W),
                         lambda h, b: (layer, h, 0, 0, 0, 0)),
            pl.BlockSpec(memory_space=pl.ANY),
        ] + [s[0] for s in cast_specs],
        out_specs=(pl.BlockSpec((DEC_SEQ, LANES), lambda h, b: (tile0 + b, h)),) + tuple(s[1] for s in cast_specs),
        scratch_shapes=[pltpu.VMEM((HEADS_PER_BLOCK, DEC_SEQ, DEC_SEQ + PAST_LEN), _bf16)],
        input_output_aliases={6: 0},
        compiler_params=pltpu.CompilerParams(
            dimension_semantics=("arbitrary", "arbitrary"), vmem_limit_bytes=VMEM_LIMIT),
        name="na_attention",
    )(proj, proj, proj, cache_k, cache_v, bias_tab, yb, *[job.w for job in cast_jobs])


MIX_TM = 512
MIX_TC = 1024
GA_COL_TILE = (2 * A_WIDTH + 3 * NA_WIDTH) // MIX_TC
GB_COL_TILE = GA_COL_TILE + D_MODEL // MIX_TC


def _mix_kernel(ga_ref, gb_ref, ya_ref, yb_ref, wpa_ref, wpb_ref, wout_ref, x_ref, gate_ref, g_ref, *rest):
    o_ref, ss_ref = rest[-2:]
    j = pl.program_id(1)

    @pl.when(j == 0)
    def _():
        o_ref[...] = jnp.zeros_like(o_ref)

    pa = jnp.dot(ya_ref[...], wpa_ref[...], preferred_element_type=_f32)
    pb = jnp.dot(yb_ref[...], wpb_ref[...], preferred_element_type=_f32)
    merged = (_sigmoid(ga_ref[...]) * pa + _sigmoid(gb_ref[...]) * pb).astype(_bf16)
    o_ref[...] += jnp.dot(merged, wout_ref[...], preferred_element_type=_f32)

    @pl.when(j == pl.num_programs(1) - 1)
    def _():
        _residual_rmsnorm_inplace(o_ref, x_ref, ss_ref, MIX_TM, gate_ref, g_ref)


def _mix(proj, ya, yb, w_pa, w_pb, w_out, x, mod, g_post, layer, tile0=0, x_out=None):
    carried = () if x_out is None else (x_out,)
    return pl.pallas_call(
        _mix_kernel,
        out_shape=jax.ShapeDtypeStruct((N_TOK, D_MODEL), _f32),
        grid=(x.shape[0] // MIX_TM, D_MODEL // MIX_TC),
        in_specs=[
            pl.BlockSpec((MIX_TM, MIX_TC), lambda i, j: (tile0 + i, GA_COL_TILE + j)),
            pl.BlockSpec((MIX_TM, MIX_TC), lambda i, j: (tile0 + i, GB_COL_TILE + j)),
            pl.BlockSpec((MIX_TM, A_WIDTH), lambda i, j: (tile0 + i, 0)),
            pl.BlockSpec((MIX_TM, NA_WIDTH), lambda i, j: (tile0 + i, 0)),
            pl.BlockSpec((A_WIDTH, MIX_TC), lambda i, j: (0, j)),
            pl.BlockSpec((NA_WIDTH, MIX_TC), lambda i, j: (0, j)),
            pl.BlockSpec((MIX_TC, D_MODEL), lambda i, j: (j, 0)),
            pl.BlockSpec((MIX_TM, D_MODEL), lambda i, j: (i, 0)),
            _mod_spec(layer, MIX_TM, 2, tile0),
            _layer_vec_spec(layer, D_MODEL),
        ] + [pl.BlockSpec(memory_space=pl.ANY)] * len(carried),
        out_specs=pl.BlockSpec((MIX_TM, D_MODEL), lambda i, j: (tile0 + i, 0)),
        scratch_shapes=[pltpu.VMEM((MIX_TM, LANES), _f32)],
        input_output_aliases={10: 0} if carried else {},
        compiler_params=pltpu.CompilerParams(
            dimension_semantics=("arbitrary", "arbitrary"), vmem_limit_bytes=VMEM_LIMIT),
        name="mix_out",
    )(proj, proj, ya, yb, w_pa, w_pb, w_out, x, mod, g_post, *carried)


FFN_TM = 1024
FFN_TF = 512


FFN_OUT_SPLIT = 2


def _ffn_kernel(n_cast, x_ref, shift_ref, scale_ref, gate_ref, gpre_ref, gpost_ref, w1_ref, w2_ref, *rest):
    o_ref, h_ref, ss_ref = rest[n_cast], rest[-2], rest[-1]
    _run_cast_jobs(rest[:n_cast], rest[n_cast + 1:-2])
    k = pl.program_id(1)

    @pl.when(k == 0)
    def _():
        def zero_out(rows, cols):
            o_ref[rows, cols] = jnp.zeros((NORM_ROWS, LANES), _f32)
        _modnorm_to(x_ref, h_ref, ss_ref, FFN_TM, gpre_ref, scale_ref, shift_ref, also=zero_out)

    a = jnp.dot(h_ref[...], w1_ref[...], preferred_element_type=_f32)
    a = jnp.square(jnp.maximum(a, 0.0)).astype(_bf16)
    width = D_MODEL // FFN_OUT_SPLIT
    for s in range(FFN_OUT_SPLIT):
        cols = slice(s * width, (s + 1) * width)
        o_ref[:, cols] += jnp.dot(a, w2_ref[:, cols], preferred_element_type=_f32)

    @pl.when(k == pl.num_programs(1) - 1)
    def _():
        _residual_rmsnorm_inplace(o_ref, x_ref, ss_ref, FFN_TM, gate_ref, gpost_ref)


FFN_K_STEPS = D_FF // FFN_TF
FFN_CAST_TILES = 8
FFN_CAST_STEPS = FFN_CAST_TILES * FFN_K_STEPS


def _ffn_cast_step(i, k):
    return jnp.where(i < FFN_CAST_TILES, i * FFN_K_STEPS + k, FFN_CAST_STEPS - 1)


def _ffn(x, mod, g_pre, g_post, w1, w2, layer, tile0=0, n_tiles=N_TOK // FFN_TM, cast_jobs=()):
    assert not cast_jobs or n_tiles >= FFN_CAST_TILES
    cast_specs = [job.specs(_ffn_cast_step) for job in cast_jobs]
    outs = pl.pallas_call(
        functools.partial(_ffn_kernel, len(cast_jobs)),
        out_shape=(jax.ShapeDtypeStruct((n_tiles * FFN_TM, D_MODEL), _f32),)
        + tuple(job.out_shape for job in cast_jobs),
        grid=(n_tiles, FFN_K_STEPS),
        in_specs=[
            pl.BlockSpec((FFN_TM, D_MODEL), lambda i, k: (tile0 + i, 0)),
            _mod_spec(layer, FFN_TM, 3, tile0), _mod_spec(layer, FFN_TM, 4, tile0), _mod_spec(layer, FFN_TM, 5, tile0),
            _layer_vec_spec(layer, D_MODEL),
            _layer_vec_spec(layer, D_MODEL),
            pl.BlockSpec((D_MODEL, FFN_TF), lambda i, k: (0, k)),
            pl.BlockSpec((FFN_TF, D_MODEL), lambda i, k: (k, 0)),
        ] + [s[0] for s in cast_specs],
        out_specs=(pl.BlockSpec((FFN_TM, D_MODEL), lambda i, k: (i, 0)),) + tuple(s[1] for s in cast_specs),
        scratch_shapes=[pltpu.VMEM((FFN_TM, D_MODEL), _bf16), pltpu.VMEM((FFN_TM, LANES), _f32)],
        compiler_params=pltpu.CompilerParams(
            dimension_semantics=("arbitrary", "arbitrary"), vmem_limit_bytes=VMEM_LIMIT),
        name="ffn",
    )(x, mod, mod, mod, g_pre, g_post, w1, w2, *[job.w for job in cast_jobs])
    return outs


def kernel(x_prompt, x_sample, cache_ctx_k, cache_ctx_v, c, c_ctx, w_mod, b_mod, g_pre_mix, g_post_mix, g_pre_ffn,
           g_post_ffn, w_in, sgu_ln_g, sgu_ln_b, sgu_w, sgu_b, na_rpb, w_pa, w_pb, w_out, w_ff1, w_ff2):
    x_streams = (x_prompt.reshape(N_CTX_TOK, D_MODEL), x_sample.reshape(DEC_BATCH * DEC_SEQ, D_MODEL))
    cvec =jnp.concatenate([c_ctx[None, :], c, jnp.zeros((MOD_ROWS - 1 - DEC_BATCH, D_MODEL), _f32)], axis=0)
    mod = _modulation(cvec, w_mod, b_mod).reshape(DEPTH, MOD_ROWS, 1, N_MOD * D_MODEL)
    bias = _na_bias_tables(na_rpb)
    cos_t, sin_t = _rope_tables()
    gdim = A_WIDTH // A_GROUPS
    sgu_b_lanes = jnp.broadcast_to(sgu_b[:, :, :, None], (DEPTH, A_GROUPS, CHUNK, gdim))

    vec = lambda a: a[:, None, :]
    g_pre_mix, g_post_mix, g_pre_ffn, g_post_ffn = vec(g_pre_mix), vec(g_post_mix), vec(g_pre_ffn), vec(g_post_ffn)
    sgu_ln_g, sgu_ln_b = vec(sgu_ln_g), vec(sgu_ln_b)
    sgu_w = sgu_w.astype(_bf16)

    states = ()
    for l in range(DEPTH):
        if l == 0:
            w_in_l = w_in
            ctx_jobs = [_CastJob(w_ff1, 0, CTX_STEPS, split_rows=False), _CastJob(w_ff2, 0, CTX_STEPS, split_rows=True)]
            na_jobs = [_CastJob(w, 0, NA_STEPS, split_rows=True) for w in (w_pa, w_pb, w_out)]
        else:
            w_in_l, ctx_jobs, na_jobs = w_in_b[None], [], []
        streams = x_streams if l == 0 else (x,)
        proj, row0 = None, 0
        for xs in streams:
            proj = _in_proj(xs, mod, g_pre_mix, w_in_l, 0, cos_t, sin_t, l, tile0=row0 // IN_TM, proj=proj)
            row0 += xs.shape[0]
        ya = _sgu(proj, sgu_ln_g, sgu_ln_b, sgu_w, sgu_b_lanes, l)
        yb, k_state, v_state, *ctx_cast = _ctx_attention(proj, l, states, ctx_jobs)
        states = (k_state, v_state)
        yb, *na_cast = _na_attention(proj, cache_ctx_k, cache_ctx_v, bias, l, yb, na_jobs)
        if l == 0:
            (w_ff1_b, w_ff2_b), (w_pa_b, w_pb_b, w_out_b) = ctx_cast, na_cast
        x, row0 = None, 0
        for xs in streams:
            x = _mix(proj, ya, yb, w_pa_b, w_pb_b, w_out_b, xs, mod, g_post_mix, l, tile0=row0 // MIX_TM, x_out=x)
            row0 += xs.shape[0]
        if l + 1 < DEPTH:
            nxt = l + 1
            jobs = [_CastJob(w, nxt, FFN_CAST_STEPS, split_rows=True) for w in (w_in, w_ff1, w_ff2, w_out)]
            jobs += [_CastJob(w, nxt, FFN_CAST_STEPS // 2, split_rows=True, steps_per_slab=2) for w in (w_pa, w_pb)]
            x, w_in_b, w_ff1_b, w_ff2_b, w_out_b, w_pa_b, w_pb_b = _ffn(
                x, mod, g_pre_ffn, g_post_ffn, w_ff1_b, w_ff2_b, l, cast_jobs=jobs)
        else:
            n_ctx_tiles = N_CTX_TOK // FFN_TM
            last = functools.partial(_ffn, x, mod, g_pre_ffn, g_post_ffn, w_ff1_b, w_ff2_b, l)
            y_prompt, = last(tile0=0, n_tiles=n_ctx_tiles)
            y_sample, = last(tile0=n_ctx_tiles, n_tiles=N_TOK // FFN_TM - n_ctx_tiles)
    k_state, v_state = (jnp.swapaxes(s, -1, -2) for s in states)
    return (y_prompt.reshape(BATCH, SEQ, D_MODEL), y_sample.reshape(DEC_BATCH, DEC_SEQ, D_MODEL), k_state, v_state)
```
